```python
import math
import jax, jax.numpy as jnp
from jax import lax
import numpy as np

D_MODEL = 2048
BATCH = 2
SEQ = 4096
DEPTH = 1
DEC_BATCH = 128
DEC_SEQ = 8
PAST_LEN = 16384
PAGE_SIZE = 128

N_HEADS = 8
QK_NOPE_DIM = 128
QK_ROPE_DIM = 64
V_HEAD_DIM = 128
Q_RANK = 512
KV_RANK = 256
MLA_WIDTH = N_HEADS * V_HEAD_DIM
MLA_SCALE = (QK_NOPE_DIM + QK_ROPE_DIM) ** -0.5
ROPE_THETA = 10000.0
Q_BLOCK = 128
POOL_GROUPS = 4
POOL_WINDOWS = (2, 4, 8, 16)
POOL_GROUP_DIM = 128
POOL_WIDTH = POOL_GROUPS * POOL_GROUP_DIM
POOL_STATE_LEN = 15
MEM_TOKENS = 256
MEM_HEADS = 4
MEM_HEAD_DIM = 128
MEM_WIDTH = MEM_HEADS * MEM_HEAD_DIM
D_MIX = MLA_WIDTH + POOL_WIDTH + MEM_WIDTH
D_IN = Q_RANK + KV_RANK + QK_ROPE_DIM + POOL_WIDTH + MEM_WIDTH + D_MIX
SPLIT_POINTS = (Q_RANK, Q_RANK + KV_RANK, Q_RANK + KV_RANK + QK_ROPE_DIM,
                Q_RANK + KV_RANK + QK_ROPE_DIM + POOL_WIDTH,
                Q_RANK + KV_RANK + QK_ROPE_DIM + POOL_WIDTH + MEM_WIDTH)
RMS_EPS = 1e-6

kernel_name = "hymba_mla_pool_memory_decode_step"


def rms_norm(x, g):
    xf = x.astype(jnp.float32)
    y = xf * lax.rsqrt(jnp.mean(xf * xf, axis=-1, keepdims=True) + RMS_EPS)
    return (y * g.astype(jnp.float32)).astype(x.dtype)


def rope_tables(pos):
    inv = ROPE_THETA ** (-jnp.arange(0, QK_ROPE_DIM, 2, dtype=jnp.float32) / QK_ROPE_DIM)
    ang = pos.astype(jnp.float32)[:, None] * inv[None, :]
    return jnp.cos(ang), jnp.sin(ang)


def apply_rope(x, cos, sin):
    x1, x2 = jnp.split(x.astype(jnp.float32), 2, axis=-1)
    return jnp.concatenate([x1 * cos - x2 * sin, x1 * sin + x2 * cos], axis=-1).astype(x.dtype)


def project(x, pos, attn_norm, w_in, q_a_norm, w_q_b, kv_a_norm, w_k_b):
    h = rms_norm(x, attn_norm)
    z = jnp.einsum('btd,de->bte', h, w_in)
    q_a, ckv, krope, u, qm, gate = jnp.split(z, SPLIT_POINTS, axis=-1)
    q = jnp.einsum('btr,rhe->bthe', rms_norm(q_a, q_a_norm), w_q_b)
    q_nope, q_rope = q[..., :QK_NOPE_DIM], q[..., QK_NOPE_DIM:]
    cos, sin = rope_tables(pos)
    q_rope = apply_rope(q_rope, cos[:, None, :], sin[:, None, :]) * MLA_SCALE
    krope = apply_rope(krope, cos, sin)
    ckv = rms_norm(ckv, kv_a_norm)
    q_lat = jnp.einsum('bthe,rhe->bthr', q_nope, w_k_b) * MLA_SCALE
    return q_lat, q_rope, ckv, krope, u, qm, gate


def mla_attend(q_lat, q_rope, ckv, krope, mask):
    s = (jnp.einsum('bthr,bsr->bhts', q_lat, ckv)
         + jnp.einsum('bthe,bse->bhts', q_rope, krope)).astype(jnp.float32)
    s = jnp.where(mask[None, None], s, -jnp.inf)
    p = jax.nn.softmax(s, axis=-1).astype(ckv.dtype)
    return jnp.einsum('bhts,bsr->bthr', p, ckv)


def prompt_mla(q_lat, q_rope, ckv, krope):
    b, t = q_lat.shape[0], q_lat.shape[1]
    key_pos = jnp.arange(t)

    def block(i):
        start = i * Q_BLOCK
        ql = lax.dynamic_slice_in_dim(q_lat, start, Q_BLOCK, axis=1)
        qr = lax.dynamic_slice_in_dim(q_rope, start, Q_BLOCK, axis=1)
        qpos = start + jnp.arange(Q_BLOCK)
        mask = key_pos[None, :] <= qpos[:, None]
        return mla_attend(ql, qr, ckv, krope, mask)

    o = lax.map(block, jnp.arange(t // Q_BLOCK))
    return jnp.moveaxis(o, 0, 1).reshape(b, t, N_HEADS, KV_RANK)


def sample_mla(q_lat, q_rope, ckv_new, krope_new, cache_ckv, cache_krope, page_table, layer):
    key_pos = jnp.arange(PAST_LEN + DEC_SEQ)
    q_pos = PAST_LEN + jnp.arange(DEC_SEQ)
    mask = key_pos[None, :] <= q_pos[:, None]

    def one_seq(args):
        ql, qr, ck_new, kr_new, pages = args
        ck = jnp.concatenate([cache_ckv[layer, pages].reshape(-1, KV_RANK).astype(ck_new.dtype), ck_new], axis=0)
        kr = jnp.concatenate([cache_krope[layer, pages].reshape(-1, QK_ROPE_DIM).astype(kr_new.dtype), kr_new], axis=0)
        return mla_attend(ql[None], qr[None], ck[None], kr[None], mask)[0]

    return lax.map(one_seq, (q_lat, q_rope, ckv_new, krope_new, page_table))


def pool_mix(u_ext, n_new, w_pool, pool_scale):
    b, length, _ = u_ext.shape
    uf = u_ext.astype(jnp.float32).reshape(b, length, POOL_GROUPS, POOL_GROUP_DIM)
    cs = jnp.concatenate([jnp.zeros_like(uf[:, :1]), jnp.cumsum(uf, axis=1)], axis=1)
    rows = jnp.arange(length - n_new, length)
    hi = cs[:, rows + 1]
    means = []
    for g, w in enumerate(POOL_WINDOWS):
        lo = jnp.maximum(rows + 1 - w, 0)
        cnt = (rows + 1 - lo).astype(jnp.float32)[None, :, None]
        means.append((hi[:, :, g] - cs[:, lo, g]) / cnt)
    diff = (jnp.stack(means, axis=2) - uf[:, length - n_new:]).astype(u_ext.dtype)
    y = jnp.einsum('btgc,gcd->btgd', diff, w_pool).reshape(b, n_new, POOL_WIDTH)
    return y * pool_scale


def mem_kv(mem, mem_norm, w_mem_kv):
    kv = jnp.einsum('bmd,de->bme', rms_norm(mem, mem_norm), w_mem_kv)
    k, v = jnp.split(kv, 2, axis=-1)
    b, m = mem.shape[0], mem.shape[1]
    return (k.reshape(b, m, MEM_HEADS, MEM_HEAD_DIM), v.reshape(b, m, MEM_HEADS, MEM_HEAD_DIM))


def cross_attend(qm, k, v):
    b, t = qm.shape[0], qm.shape[1]
    q = qm.reshape(b, t, MEM_HEADS, MEM_HEAD_DIM) * (MEM_HEAD_DIM ** -0.5)
    s = jnp.einsum('bthd,bmhd->bhtm', q, k.astype(q.dtype)).astype(jnp.float32)
    p = jax.nn.softmax(s, axis=-1).astype(q.dtype)
    return jnp.einsum('bhtm,bmhd->bthd', p, v.astype(q.dtype)).reshape(b, t, MEM_WIDTH)


def combine(x, o_lat, pool_out, cross_out, gate, w_v_b, w_out):
    b, t = x.shape[0], x.shape[1]
    mla_out = jnp.einsum('bthr,rhd->bthd', o_lat, w_v_b).reshape(b, t, MLA_WIDTH)
    mix = jnp.concatenate([mla_out, pool_out, cross_out], axis=-1) * jax.nn.silu(gate)
    return x + jnp.einsum('bte,ed->btd', mix, w_out)


def setup_inputs(seed: int = 0) -> dict:
    key = jax.random.key(seed)
    ks = jax.random.split(key, 24)
    n_pages = PAST_LEN // PAGE_SIZE
    n_used = DEC_BATCH * n_pages
    n_phys = (n_used * 5) // 4
    nrm = lambda k, shape, scale=1.0: jax.random.normal(k, shape, jnp.float32) * scale
    page_table = jax.random.permutation(ks[5], n_phys)[:n_used].reshape(DEC_BATCH, n_pages).astype(jnp.int32)
    return {
        "x_prompt": nrm(ks[0], (BATCH, SEQ, D_MODEL)),
        "x_sample": nrm(ks[1], (DEC_BATCH, DEC_SEQ, D_MODEL)),
        "mem_prompt": nrm(ks[2], (BATCH, MEM_TOKENS, D_MODEL)),
        "cache_ckv": nrm(ks[3], (DEPTH, n_phys, PAGE_SIZE, KV_RANK)),
        "cache_krope": nrm(ks[4], (DEPTH, n_phys, PAGE_SIZE, QK_ROPE_DIM)),
        "page_table": page_table,
        "state_pool": nrm(ks[6], (DEPTH, DEC_BATCH, POOL_STATE_LEN, POOL_WIDTH)),
        "cache_mem_k": nrm(ks[7], (DEPTH, DEC_BATCH, MEM_TOKENS, MEM_HEADS, MEM_HEAD_DIM)),
        "cache_mem_v": nrm(ks[8], (DEPTH, DEC_BATCH, MEM_TOKENS, MEM_HEADS, MEM_HEAD_DIM)),
        "attn_norm": 1.0 + nrm(ks[9], (DEPTH, D_MODEL), 0.05),
        "mem_norm": 1.0 + nrm(ks[10], (DEPTH, D_MODEL), 0.05),
        "w_in": nrm(ks[11], (DEPTH, D_MODEL, D_IN), D_MODEL ** -0.5),
        "q_a_norm": 1.0 + nrm(ks[12], (DEPTH, Q_RANK), 0.05),
        "w_q_b": nrm(ks[13], (DEPTH, Q_RANK, N_HEADS, QK_NOPE_DIM + QK_ROPE_DIM), Q_RANK ** -0.5),
        "kv_a_norm": 1.0 + nrm(ks[14], (DEPTH, KV_RANK), 0.05),
        "w_k_b": nrm(ks[15], (DEPTH, KV_RANK, N_HEADS, QK_NOPE_DIM), KV_RANK ** -0.5),
        "w_v_b": nrm(ks[16], (DEPTH, KV_RANK, N_HEADS, V_HEAD_DIM), KV_RANK ** -0.5),
        "w_pool": nrm(ks[17], (DEPTH, POOL_GROUPS, POOL_GROUP_DIM, POOL_GROUP_DIM), POOL_GROUP_DIM ** -0.5),
        "pool_scale": 1.0 + nrm(ks[18], (DEPTH, POOL_WIDTH), 0.05),
        "w_mem_kv": nrm(ks[19], (DEPTH, D_MODEL, 2 * MEM_WIDTH), D_MODEL ** -0.5),
        "w_out": nrm(ks[20], (DEPTH, D_MIX, D_MODEL), D_MIX ** -0.5),
        "final_norm": 1.0 + nrm(ks[21], (D_MODEL,), 0.05),
    }


def reference(x_prompt, x_sample, mem_prompt, cache_ckv, cache_krope, page_table, state_pool,
              cache_mem_k, cache_mem_v, attn_norm, mem_norm, w_in, q_a_norm, w_q_b, kv_a_norm,
              w_k_b, w_v_b, w_pool, pool_scale, w_mem_kv, w_out, final_norm):
    pos_p = jnp.arange(SEQ)
    pos_s = PAST_LEN + jnp.arange(DEC_SEQ)
    xp, xs = x_prompt, x_sample
    ckv_p, kr_p, pool_p, mk_p, mv_p, ckv_s, kr_s, pool_s = [], [], [], [], [], [], [], []
    for l in range(DEPTH):
        q_lat, q_rope, ckv, krope, u, qm, gate = project(
            xp, pos_p, attn_norm[l], w_in[l], q_a_norm[l], w_q_b[l], kv_a_norm[l], w_k_b[l])
        o_lat = prompt_mla(q_lat, q_rope, ckv, krope)
        pool_out = pool_mix(u, SEQ, w_pool[l], pool_scale[l])
        mk, mv = mem_kv(mem_prompt, mem_norm[l], w_mem_kv[l])
        cross = cross_attend(qm, mk, mv)
        xp = combine(xp, o_lat, pool_out, cross, gate, w_v_b[l], w_out[l])
        ckv_p.append(ckv)
        kr_p.append(krope)
        pool_p.append(u[:, SEQ - POOL_STATE_LEN:])
        mk_p.append(mk)
        mv_p.append(mv)
        q_lat, q_rope, ckv, krope, u, qm, gate = project(
            xs, pos_s, attn_norm[l], w_in[l], q_a_norm[l], w_q_b[l], kv_a_norm[l], w_k_b[l])
        o_lat = sample_mla(q_lat, q_rope, ckv, krope, cache_ckv, cache_krope, page_table, l)
        u_ext = jnp.concatenate([state_pool[l].astype(u.dtype), u], axis=1)
        pool_out = pool_mix(u_ext, DEC_SEQ, w_pool[l], pool_scale[l])
        cross = cross_attend(qm, cache_mem_k[l], cache_mem_v[l])
        xs = combine(xs, o_lat, pool_out, cross, gate, w_v_b[l], w_out[l])
        ckv_s.append(ckv)
        kr_s.append(krope)
        pool_s.append(u_ext[:, DEC_SEQ:])
    y_prompt = rms_norm(xp, final_norm)
    y_sample = rms_norm(xs, final_norm)
    return (y_prompt, y_sample, jnp.stack(ckv_p), jnp.stack(kr_p), jnp.stack(pool_p),
            jnp.stack(mk_p), jnp.stack(mv_p), jnp.stack(ckv_s), jnp.stack(kr_s), jnp.stack(pool_s))
```

```python
import functools

import jax
import jax.numpy as jnp
from jax import lax
from jax.experimental import pallas as pl
from jax.experimental.pallas import tpu as pltpu

F32 = jnp.float32
BF16 = jnp.bfloat16

D_MODEL = 2048
BATCH = 2
SEQ = 4096
DEPTH = 1
DEC_BATCH = 128
DEC_SEQ = 8
PAST_LEN = 16384
PAGE_SIZE = 128
N_PAGES = PAST_LEN // PAGE_SIZE

N_HEADS = 8
QK_NOPE_DIM = 128
QK_ROPE_DIM = 64
V_HEAD_DIM = 128
Q_RANK = 512
KV_RANK = 256
MLA_WIDTH = N_HEADS * V_HEAD_DIM
MLA_SCALE = (QK_NOPE_DIM + QK_ROPE_DIM) ** -0.5
ROPE_THETA = 10000.0
POOL_GROUPS = 4
POOL_WINDOWS = (2, 4, 8, 16)
POOL_GROUP_DIM = 128
POOL_WIDTH = POOL_GROUPS * POOL_GROUP_DIM
POOL_STATE_LEN = 15
MEM_TOKENS = 256
MEM_HEADS = 4
MEM_HEAD_DIM = 128
MEM_WIDTH = MEM_HEADS * MEM_HEAD_DIM
MEM_SCALE = MEM_HEAD_DIM ** -0.5
D_MIX = MLA_WIDTH + POOL_WIDTH + MEM_WIDTH
RMS_EPS = 1e-6

LANE = 128
ROPE_PAD = LANE
QK_WIDTH = KV_RANK + ROPE_PAD
C_QA = 0
C_CKV = C_QA + Q_RANK
C_KR = C_CKV + KV_RANK
C_U = C_KR + 2 * ROPE_PAD
C_QM = C_U + POOL_WIDTH
C_GATE = C_QM + MEM_WIDTH
C_END = C_GATE + D_MIX
Q_NOPE_W = N_HEADS * QK_NOPE_DIM
Q_ROPE_W = N_HEADS * ROPE_PAD

VMEM_LIMIT = 56 * 1024 * 1024
NEG_BIG = -1e30

PROJ_ROWS = 256
ATT_Q = 128
ATT_K = 512
PAGES_PER_STEP = 16
CROSS_SEQS = 4


def _const_spec(shape):
    n = len(shape)
    return pl.BlockSpec(shape, lambda *_: (0,) * n, pipeline_mode=pl.Buffered(1))


def _params(n_axes):
    return pltpu.CompilerParams(dimension_semantics=("arbitrary",) * n_axes,
                                vmem_limit_bytes=VMEM_LIMIT)


def _rms(x, g):
    return x * lax.rsqrt(jnp.mean(x * x, axis=-1, keepdims=True) + RMS_EPS) * g


def _dot(a, b):
    return jnp.dot(a, b, preferred_element_type=F32)


def _dot_nt(a, b):
    return lax.dot_general(a, b, (((1,), (1,)), ((), ())), preferred_element_type=F32)


def _softmax_rows(s):
    m = jnp.max(s, axis=-1, keepdims=True)
    e = jnp.exp(s - m)
    return e / jnp.sum(e, axis=-1, keepdims=True)


def _mem_kv_kernel(mem_ref, g_ref, w_ref, k_ref, v_ref, kb_ref, vb_ref):
    h = _rms(mem_ref[...], g_ref[...]).astype(BF16)
    kv = _dot(h, w_ref[...])
    k = kv[:, :MEM_WIDTH]
    v = kv[:, MEM_WIDTH:]
    k_ref[...] = k
    v_ref[...] = v
    kb_ref[...] = k.astype(BF16)
    vb_ref[...] = v.astype(BF16)


def _mem_kv(mem, mem_norm, w_mem_kv):
    b = mem.shape[0]
    out_f32 = jax.ShapeDtypeStruct((b, MEM_TOKENS, MEM_WIDTH), F32)
    out_bf16 = jax.ShapeDtypeStruct((b, MEM_TOKENS, MEM_WIDTH), BF16)
    blk = pl.BlockSpec((None, MEM_TOKENS, MEM_WIDTH), lambda i: (i, 0, 0))
    return pl.pallas_call(
        _mem_kv_kernel,
        grid=(b,),
        in_specs=[pl.BlockSpec((None, MEM_TOKENS, D_MODEL), lambda i: (i, 0, 0)),
                  _const_spec((1, D_MODEL)),
                  _const_spec((D_MODEL, 2 * MEM_WIDTH))],
        out_specs=[blk, blk, blk, blk],
        out_shape=[out_f32, out_f32, out_bf16, out_bf16],
        compiler_params=_params(1),
        name="mem_kv",
    )(mem, mem_norm, w_mem_kv)


def _project(x, cs, sn, an_ref, w1_ref, qan_ref, wq_ref, kvn_ref, wkb_ref):
    h = _rms(x, an_ref[...]).astype(BF16)

    def seg(a, b):
        return _dot(h, w1_ref[:, a:b])

    qa = _rms(seg(C_QA, C_CKV), qan_ref[...]).astype(BF16)
    q_nope = _dot(qa, wq_ref[:, 0:Q_NOPE_W])
    q_rope = _dot(qa, wq_ref[:, Q_NOPE_W:Q_NOPE_W + Q_ROPE_W])
    q_swap = _dot(qa, wq_ref[:, Q_NOPE_W + Q_ROPE_W:Q_NOPE_W + 2 * Q_ROPE_W])
    heads = []
    for hh in range(N_HEADS):
        sl = slice(hh * LANE, (hh + 1) * LANE)
        q_lat = _dot(q_nope[:, sl].astype(BF16), wkb_ref[hh]) * MLA_SCALE
        q_r = (q_rope[:, sl] * cs + q_swap[:, sl] * sn) * MLA_SCALE
        heads.append((q_lat, q_r))
    ckv = _rms(seg(C_CKV, C_KR), kvn_ref[...])
    zk = seg(C_KR, C_U)
    kr = zk[:, :ROPE_PAD] * cs + zk[:, ROPE_PAD:] * sn
    u = seg(C_U, C_QM)
    qm = seg(C_QM, C_GATE)
    g = seg(C_GATE, C_END)
    sg = g / (1.0 + jnp.exp(-g))
    return heads, ckv, kr, u, qm, sg


def _proj_prompt_kernel(x_ref, cs_ref, sn_ref, an_ref, w1_ref, qan_ref, wq_ref, kvn_ref, wkb_ref,
                        wpool_ref, pscale_ref, mk_ref, mv_ref,
                        q_ref, ckv_ref, kr_ref, kc_ref, pool_ref, cross_ref, sg_ref, pst_ref,
                        uext_ref):
    j = pl.program_id(1)
    tm = PROJ_ROWS
    halo = 16
    heads, ckv, kr, u, qm, sg = _project(
        x_ref[...], cs_ref[...], sn_ref[...], an_ref, w1_ref, qan_ref, wq_ref, kvn_ref, wkb_ref)

    for hh, (q_lat, q_r) in enumerate(heads):
        q_ref[hh, :, 0:KV_RANK] = q_lat.astype(BF16)
        q_ref[hh, :, KV_RANK:QK_WIDTH] = q_r.astype(BF16)
    ckv_ref[...] = ckv
    kr_ref[...] = kr[:, :QK_ROPE_DIM]
    kc_ref[:, 0:KV_RANK] = ckv.astype(BF16)
    kc_ref[:, KV_RANK:QK_WIDTH] = kr.astype(BF16)
    sg_ref[...] = sg.astype(BF16)

    @pl.when(j == 0)
    def _():
        uext_ref[0:halo, :] = jnp.zeros((halo, POOL_WIDTH), F32)

    uext_ref[halo:halo + tm, :] = u
    pos = j * tm + lax.broadcasted_iota(jnp.int32, (tm, 1), 0)
    for g, w in enumerate(POOL_WINDOWS):
        sl = slice(g * POOL_GROUP_DIM, (g + 1) * POOL_GROUP_DIM)
        acc = u[:, sl]
        for k in range(1, w):
            acc = acc + uext_ref[halo - k:halo - k + tm, sl]
        cnt = jnp.minimum(pos + 1, w).astype(F32)
        diff = acc / cnt - u[:, sl]
        y = _dot(diff.astype(BF16), wpool_ref[g]) * pscale_ref[:, sl]
        pool_ref[:, sl] = y.astype(BF16)

    @pl.when(j == pl.num_programs(1) - 1)
    def _():
        pst_ref[...] = uext_ref[halo + tm - POOL_STATE_LEN:halo + tm, :]

    uext_ref[0:halo, :] = uext_ref[tm:tm + halo, :]

    for hh in range(MEM_HEADS):
        sl = slice(hh * MEM_HEAD_DIM, (hh + 1) * MEM_HEAD_DIM)
        s = _dot_nt(qm[:, sl].astype(BF16), mk_ref[:, sl]) * MEM_SCALE
        p = _softmax_rows(s).astype(BF16)
        cross_ref[:, sl] = _dot(p, mv_ref[:, sl]).astype(BF16)


def _proj_prompt(x, cs, sn, wts, mkb, mvb):
    b, t, _ = x.shape
    tm = PROJ_ROWS
    nj = t // tm

    def rows(width, dtype):
        return (pl.BlockSpec((None, tm, width), lambda i, j: (i, j, 0)),
                jax.ShapeDtypeStruct((b, t, width), dtype))

    outs = [
        (pl.BlockSpec((None, N_HEADS, tm, QK_WIDTH), lambda i, j: (i, 0, j, 0)),
         jax.ShapeDtypeStruct((b, N_HEADS, t, QK_WIDTH), BF16)),
        rows(KV_RANK, F32),
        rows(QK_ROPE_DIM, F32),
        rows(QK_WIDTH, BF16),
        rows(POOL_WIDTH, BF16),
        rows(MEM_WIDTH, BF16),
        rows(D_MIX, BF16),
        (pl.BlockSpec((None, POOL_STATE_LEN, POOL_WIDTH), lambda i, j: (i, 0, 0)),
         jax.ShapeDtypeStruct((b, POOL_STATE_LEN, POOL_WIDTH), F32)),
    ]
    mem_spec = pl.BlockSpec((None, MEM_TOKENS, MEM_WIDTH), lambda i, j: (i, 0, 0))
    in_specs = [
        pl.BlockSpec((None, tm, D_MODEL), lambda i, j: (i, j, 0)),
        pl.BlockSpec((tm, ROPE_PAD), lambda i, j: (j, 0)),
        pl.BlockSpec((tm, ROPE_PAD), lambda i, j: (j, 0)),
    ] + [_const_spec(w.shape) for w in wts] + [mem_spec, mem_spec]
    return pl.pallas_call(
        _proj_prompt_kernel,
        grid=(b, nj),
        in_specs=in_specs,
        out_specs=[o[0] for o in outs],
        out_shape=[o[1] for o in outs],
        scratch_shapes=[pltpu.VMEM((16 + tm, POOL_WIDTH), F32)],
        compiler_params=_params(2),
        name="proj_prompt",
    )(x, cs, sn, *wts, mkb, mvb)


def _attn_prompt_kernel(q_ref, kc_ref, wvb_ref, o_ref, m_ref, l_ref, acc_ref):
    i = pl.program_id(1)
    rows = N_HEADS * ATT_Q
    q = q_ref[...].reshape(rows, QK_WIDTH)
    m_ref[...] = jnp.full((rows, 1), NEG_BIG, F32)
    l_ref[...] = jnp.zeros((rows, 1), F32)
    acc_ref[...] = jnp.zeros((rows, KV_RANK), F32)

    def step(c, masked):
        start = pl.multiple_of(c * ATT_K, ATT_K)
        k = kc_ref[pl.ds(start, ATT_K), :]
        s = _dot_nt(q, k)
        if masked:
            key_pos = start + lax.broadcasted_iota(jnp.int32, (ATT_Q, ATT_K), 1)
            q_pos = i * ATT_Q + lax.broadcasted_iota(jnp.int32, (ATT_Q, ATT_K), 0)
            s = jnp.where((key_pos <= q_pos)[None], s.reshape(N_HEADS, ATT_Q, ATT_K), NEG_BIG)
            s = s.reshape(rows, ATT_K)
        m_prev = m_ref[...]
        m_new = jnp.maximum(m_prev, jnp.max(s, axis=-1, keepdims=True))
        alpha = jnp.exp(m_prev - m_new)
        p = jnp.exp(s - m_new)
        l_ref[...] = alpha * l_ref[...] + jnp.sum(p, axis=-1, keepdims=True)
        acc_ref[...] = alpha * acc_ref[...] + _dot(p.astype(BF16), k[:, 0:KV_RANK])
        m_ref[...] = m_new

    n_full = (i * ATT_Q) // ATT_K

    def body(c, carry):
        step(c, masked=False)
        return carry

    lax.fori_loop(0, n_full, body, 0)
    step(n_full, masked=True)

    o = (acc_ref[...] / l_ref[...]).astype(BF16)
    for hh in range(N_HEADS):
        o_h = o[hh * ATT_Q:(hh + 1) * ATT_Q, :]
        o_ref[:, hh * V_HEAD_DIM:(hh + 1) * V_HEAD_DIM] = _dot(o_h, wvb_ref[hh]).astype(BF16)


def _attn_prompt(q, kc, wvb):
    b, _, t, _ = q.shape
    rows = N_HEADS * ATT_Q
    return pl.pallas_call(
        _attn_prompt_kernel,
        grid=(b, t // ATT_Q),
        in_specs=[pl.BlockSpec((None, N_HEADS, ATT_Q, QK_WIDTH), lambda bi, i: (bi, 0, i, 0)),
                  pl.BlockSpec((None, t, QK_WIDTH), lambda bi, i: (bi, 0, 0)),
                  _const_spec(wvb.shape)],
        out_specs=pl.BlockSpec((None, ATT_Q, MLA_WIDTH), lambda bi, i: (bi, i, 0)),
        out_shape=jax.ShapeDtypeStruct((b, t, MLA_WIDTH), BF16),
        scratch_shapes=[pltpu.VMEM((rows, 1), F32), pltpu.VMEM((rows, 1), F32),
                        pltpu.VMEM((rows, KV_RANK), F32)],
        compiler_params=_params(2),
        name="attn_prompt",
    )(q, kc, wvb)


def _combine_kernel(x_ref, mla_ref, pool_ref, cross_ref, sg_ref, wout_ref, fn_ref, y_ref):
    def part(ref, a, b):
        mix = ref[...].astype(F32) * sg_ref[:, a:b].astype(F32)
        return _dot(mix.astype(BF16), wout_ref[a:b, :])

    acc = x_ref[...] + part(mla_ref, 0, MLA_WIDTH)
    acc = acc + part(pool_ref, MLA_WIDTH, MLA_WIDTH + POOL_WIDTH)
    acc = acc + part(cross_ref, MLA_WIDTH + POOL_WIDTH, D_MIX)
    y_ref[...] = _rms(acc, fn_ref[...])


def _combine(x, mla, pool, cross, sg, wout, fn):
    n = x.shape[0]
    tm = PROJ_ROWS

    def rows(width):
        return pl.BlockSpec((tm, width), lambda i: (i, 0))

    return pl.pallas_call(
        _combine_kernel,
        grid=(n // tm,),
        in_specs=[rows(D_MODEL), rows(MLA_WIDTH), rows(POOL_WIDTH), rows(MEM_WIDTH), rows(D_MIX),
                  _const_spec(wout.shape), _const_spec(fn.shape)],
        out_specs=rows(D_MODEL),
        out_shape=jax.ShapeDtypeStruct((n, D_MODEL), F32),
        compiler_params=_params(1),
        name="combine",
    )(x, mla, pool, cross, sg, wout, fn)


def _proj_sample_kernel(x_ref, cs_ref, sn_ref, an_ref, w1_ref, qan_ref, wq_ref, kvn_ref, wkb_ref,
                        wpool_ref, pscale_ref, st_ref,
                        q_ref, ckv_ref, kr_ref, kc_ref, pool_ref, qm_ref, sg_ref, pst_ref,
                        ext_ref):
    tm = PROJ_ROWS
    ns = tm // DEC_SEQ
    cs = jnp.broadcast_to(cs_ref[...][None], (ns, DEC_SEQ, ROPE_PAD)).reshape(tm, ROPE_PAD)
    sn = jnp.broadcast_to(sn_ref[...][None], (ns, DEC_SEQ, ROPE_PAD)).reshape(tm, ROPE_PAD)
    heads, ckv, kr, u, qm, sg = _project(
        x_ref[...], cs, sn, an_ref, w1_ref, qan_ref, wq_ref, kvn_ref, wkb_ref)

    for hh, (q_lat, q_r) in enumerate(heads):
        q_ref[:, hh, :, 0:KV_RANK] = q_lat.reshape(ns, DEC_SEQ, KV_RANK)
        q_ref[:, hh, :, KV_RANK:QK_WIDTH] = q_r.reshape(ns, DEC_SEQ, ROPE_PAD)
    ckv_ref[...] = ckv
    kr_ref[...] = kr[:, :QK_ROPE_DIM]
    kc_ref[:, 0:KV_RANK] = ckv
    kc_ref[:, KV_RANK:QK_WIDTH] = kr
    qm_ref[...] = qm
    sg_ref[...] = sg.astype(BF16)

    base = POOL_STATE_LEN + 1
    ext_ref[:, 1:base, :] = st_ref[...]
    ext_ref[:, base:base + DEC_SEQ, :] = u.reshape(ns, DEC_SEQ, POOL_WIDTH)
    for g, w in enumerate(POOL_WINDOWS):
        sl = slice(g * POOL_GROUP_DIM, (g + 1) * POOL_GROUP_DIM)
        acc = ext_ref[:, base:base + DEC_SEQ, sl]
        for k in range(1, w):
            acc = acc + ext_ref[:, base - k:base - k + DEC_SEQ, sl]
        diff = (acc * (1.0 / w)).reshape(tm, POOL_GROUP_DIM) - u[:, sl]
        y = _dot(diff.astype(BF16), wpool_ref[g]) * pscale_ref[:, sl]
        pool_ref[:, sl] = y.astype(BF16)
    pst_ref[...] = ext_ref[:, base + DEC_SEQ - POOL_STATE_LEN:base + DEC_SEQ, :]


def _proj_sample(x, cs, sn, wts, state):
    n = x.shape[0]
    tm = PROJ_ROWS
    ns = tm // DEC_SEQ
    nseq = n // DEC_SEQ

    def rows(width, dtype):
        return (pl.BlockSpec((tm, width), lambda i: (i, 0)), jax.ShapeDtypeStruct((n, width), dtype))

    outs = [
        (pl.BlockSpec((ns, N_HEADS, DEC_SEQ, QK_WIDTH), lambda i: (i, 0, 0, 0)),
         jax.ShapeDtypeStruct((nseq, N_HEADS, DEC_SEQ, QK_WIDTH), F32)),
        rows(KV_RANK, F32),
        rows(QK_ROPE_DIM, F32),
        rows(QK_WIDTH, F32),
        rows(POOL_WIDTH, BF16),
        rows(MEM_WIDTH, F32),
        rows(D_MIX, BF16),
        (pl.BlockSpec((ns, POOL_STATE_LEN, POOL_WIDTH), lambda i: (i, 0, 0)),
         jax.ShapeDtypeStruct((nseq, POOL_STATE_LEN, POOL_WIDTH), F32)),
    ]
    in_specs = [
        pl.BlockSpec((tm, D_MODEL), lambda i: (i, 0)),
        _const_spec(cs.shape),
        _const_spec(sn.shape),
    ] + [_const_spec(w.shape) for w in wts] + [
        pl.BlockSpec((ns, POOL_STATE_LEN, POOL_WIDTH), lambda i: (i, 0, 0))]
    return pl.pallas_call(
        _proj_sample_kernel,
        grid=(n // tm,),
        in_specs=in_specs,
        out_specs=[o[0] for o in outs],
        out_shape=[o[1] for o in outs],
        scratch_shapes=[pltpu.VMEM((ns, POOL_STATE_LEN + 1 + DEC_SEQ, POOL_WIDTH), F32)],
        compiler_params=_params(1),
        name="proj_sample",
    )(x, cs, sn, *wts, state)


def _cross_sample_kernel(qm_ref, k_ref, v_ref, o_ref):
    rows = MEM_HEADS * DEC_SEQ
    r_head = lax.broadcasted_iota(jnp.int32, (rows, MEM_WIDTH), 0) // DEC_SEQ
    c_head = lax.broadcasted_iota(jnp.int32, (rows, MEM_WIDTH), 1) // MEM_HEAD_DIM
    head_mask = r_head == c_head
    outs = []
    for s_i in range(CROSS_SEQS):
        q = qm_ref[s_i * DEC_SEQ:(s_i + 1) * DEC_SEQ, :]
        q_bd = jnp.where(head_mask, jnp.concatenate([q] * MEM_HEADS, axis=0), 0.0).astype(BF16)
        k = k_ref[s_i].astype(BF16)
        v = v_ref[s_i].astype(BF16)
        p = _softmax_rows(_dot_nt(q_bd, k) * MEM_SCALE).astype(BF16)
        o_bd = jnp.where(head_mask, _dot(p, v), 0.0)
        o = o_bd[0:DEC_SEQ]
        for hh in range(1, MEM_HEADS):
            o = o + o_bd[hh * DEC_SEQ:(hh + 1) * DEC_SEQ]
        outs.append(o)
    o_ref[...] = jnp.concatenate(outs, axis=0).astype(BF16)


def _cross_sample(qm, mem_k, mem_v):
    n = qm.shape[0]
    tm = CROSS_SEQS * DEC_SEQ
    kv_spec = pl.BlockSpec((CROSS_SEQS, MEM_TOKENS, MEM_WIDTH), lambda i: (i, 0, 0))
    return pl.pallas_call(
        _cross_sample_kernel,
        grid=(n // tm,),
        in_specs=[pl.BlockSpec((tm, MEM_WIDTH), lambda i: (i, 0)), kv_spec, kv_spec],
        out_specs=pl.BlockSpec((tm, MEM_WIDTH), lambda i: (i, 0)),
        out_shape=jax.ShapeDtypeStruct((n, MEM_WIDTH), BF16),
        compiler_params=_params(1),
        name="cross_sample",
    )(qm, mem_k, mem_v)


def _attn_sample_kernel(pt_ref, q_ref, knew_ref, *rest):
    del pt_ref
    npg = PAGES_PER_STEP
    ckv_refs = rest[:npg]
    kr_refs = rest[npg:2 * npg]
    o_ref, m_ref, l_ref, acc_ref = rest[2 * npg:]
    c = pl.program_id(1)
    rows = N_HEADS * DEC_SEQ
    q = q_ref[...].reshape(rows, QK_WIDTH).astype(BF16)

    @pl.when(c == 0)
    def _():
        pad = jnp.zeros((PAGE_SIZE - DEC_SEQ, QK_WIDTH), F32)
        kn = jnp.concatenate([knew_ref[...], pad], axis=0).astype(BF16)
        s = _dot_nt(q, kn)
        key_j = lax.broadcasted_iota(jnp.int32, (rows, PAGE_SIZE), 1)
        q_t = lax.broadcasted_iota(jnp.int32, (rows, PAGE_SIZE), 0) % DEC_SEQ
        s = jnp.where(key_j <= q_t, s, NEG_BIG)
        m = jnp.max(s, axis=-1, keepdims=True)
        p = jnp.exp(s - m)
        m_ref[...] = m
        l_ref[...] = jnp.sum(p, axis=-1, keepdims=True)
        acc_ref[...] = _dot(p.astype(BF16), kn[:, 0:KV_RANK])

    kc = jnp.concatenate([r[...].astype(BF16) for r in ckv_refs], axis=0)
    kr = jnp.concatenate([r[...].astype(BF16) for r in kr_refs], axis=0)
    s = _dot_nt(q[:, 0:KV_RANK], kc) + _dot_nt(q[:, KV_RANK:KV_RANK + QK_ROPE_DIM], kr)
    m_prev = m_ref[...]
    m_new = jnp.maximum(m_prev, jnp.max(s, axis=-1, keepdims=True))
    alpha = jnp.exp(m_prev - m_new)
    p = jnp.exp(s - m_new)
    l_new = alpha * l_ref[...] + jnp.sum(p, axis=-1, keepdims=True)
    acc_new = alpha * acc_ref[...] + _dot(p.astype(BF16), kc)
    m_ref[...] = m_new
    l_ref[...] = l_new
    acc_ref[...] = acc_new

    @pl.when(c == pl.num_programs(1) - 1)
    def _():
        o_ref[...] = (acc_new / l_new).reshape(N_HEADS, DEC_SEQ, KV_RANK)


def _attn_sample(page_table, q, knew, cache_ckv, cache_krope):
    nseq = q.shape[0]
    npg = PAGES_PER_STEP
    rows = N_HEADS * DEC_SEQ

    def page_spec(width, j):
        return pl.BlockSpec((None, PAGE_SIZE, width),
                            lambda s, c, pt, j=j: (pt[s, c * npg + j], 0, 0))

    in_specs = [pl.BlockSpec((None, N_HEADS, DEC_SEQ, QK_WIDTH), lambda s, c, pt: (s, 0, 0, 0)),
                pl.BlockSpec((DEC_SEQ, QK_WIDTH), lambda s, c, pt: (s, 0))]
    in_specs += [page_spec(KV_RANK, j) for j in range(npg)]
    in_specs += [page_spec(QK_ROPE_DIM, j) for j in range(npg)]
    grid_spec = pltpu.PrefetchScalarGridSpec(
        num_scalar_prefetch=1,
        grid=(nseq, N_PAGES // npg),
        in_specs=in_specs,
        out_specs=pl.BlockSpec((None, N_HEADS, DEC_SEQ, KV_RANK), lambda s, c, pt: (s, 0, 0, 0)),
        scratch_shapes=[pltpu.VMEM((rows, 1), F32), pltpu.VMEM((rows, 1), F32),
                        pltpu.VMEM((rows, KV_RANK), F32)],
    )
    return pl.pallas_call(
        _attn_sample_kernel,
        grid_spec=grid_spec,
        out_shape=jax.ShapeDtypeStruct((nseq, N_HEADS, DEC_SEQ, KV_RANK), F32),
        compiler_params=_params(2),
        name="attn_sample",
    )(page_table, q, knew, *([cache_ckv] * npg), *([cache_krope] * npg))


def _vup_sample_kernel(o_ref, wvb_ref, mla_ref):
    ns = PROJ_ROWS // DEC_SEQ
    for hh in range(N_HEADS):
        o_h = o_ref[:, hh, :, :].reshape(ns * DEC_SEQ, KV_RANK).astype(BF16)
        mla_ref[:, hh * V_HEAD_DIM:(hh + 1) * V_HEAD_DIM] = _dot(o_h, wvb_ref[hh]).astype(BF16)


def _vup_sample(o_lat, wvb):
    nseq = o_lat.shape[0]
    ns = PROJ_ROWS // DEC_SEQ
    return pl.pallas_call(
        _vup_sample_kernel,
        grid=(nseq // ns,),
        in_specs=[pl.BlockSpec((ns, N_HEADS, DEC_SEQ, KV_RANK), lambda i: (i, 0, 0, 0)),
                  _const_spec(wvb.shape)],
        out_specs=pl.BlockSpec((PROJ_ROWS, MLA_WIDTH), lambda i: (i, 0)),
        out_shape=jax.ShapeDtypeStruct((nseq * DEC_SEQ, MLA_WIDTH), BF16),
        compiler_params=_params(1),
        name="vup_sample",
    )(o_lat, wvb)


def _rope_tables(pos):
    inv = ROPE_THETA ** (-jnp.arange(0, QK_ROPE_DIM, 2, dtype=F32) / QK_ROPE_DIM)
    ang = pos.astype(F32)[:, None] * inv[None, :]
    cos, sin = jnp.cos(ang), jnp.sin(ang)
    reps = ROPE_PAD // QK_ROPE_DIM
    cs = jnp.tile(jnp.concatenate([cos, cos], axis=-1), (1, reps))
    sn = jnp.tile(jnp.concatenate([-sin, sin], axis=-1), (1, reps))
    return cs, sn


def _swap_halves(w):
    half = w.shape[-1] // 2
    return jnp.concatenate([w[..., half:], w[..., :half]], axis=-1)


def _pad_lanes(w, width):
    return jnp.pad(w, [(0, 0)] * (w.ndim - 1) + [(0, width - w.shape[-1])])


def _layout_weights(attn_norm, w_in, q_a_norm, w_q_b, kv_a_norm, w_k_b, w_pool, pool_scale):
    s0 = Q_RANK
    s1 = s0 + KV_RANK
    s2 = s1 + QK_ROPE_DIM
    w_kr = w_in[:, s1:s2]
    w1 = jnp.concatenate([
        w_in[:, :s1],
        _pad_lanes(w_kr, ROPE_PAD), _pad_lanes(_swap_halves(w_kr), ROPE_PAD),
        w_in[:, s2:]], axis=1).astype(BF16)
    wq_nope = w_q_b[:, :, :QK_NOPE_DIM].reshape(Q_RANK, Q_NOPE_W)
    wq_r = w_q_b[:, :, QK_NOPE_DIM:]
    wq_rope = _pad_lanes(wq_r, ROPE_PAD).reshape(Q_RANK, Q_ROPE_W)
    wq_swap = _pad_lanes(_swap_halves(wq_r), ROPE_PAD).reshape(Q_RANK, Q_ROPE_W)
    wq = jnp.concatenate([wq_nope, wq_rope, wq_swap], axis=1).astype(BF16)
    wkb = jnp.transpose(w_k_b, (1, 2, 0)).astype(BF16)
    return (attn_norm.reshape(1, D_MODEL), w1, q_a_norm.reshape(1, Q_RANK), wq,
            kv_a_norm.reshape(1, KV_RANK), wkb, w_pool.astype(BF16),
            pool_scale.reshape(1, POOL_WIDTH))


def kernel(x_prompt, x_sample, mem_prompt, cache_ckv, cache_krope, page_table, state_pool,
           cache_mem_k, cache_mem_v, attn_norm, mem_norm, w_in, q_a_norm, w_q_b, kv_a_norm,
           w_k_b, w_v_b, w_pool, pool_scale, w_mem_kv, w_out, final_norm):
    assert DEPTH == 1 and attn_norm.shape[0] == 1
    n_phys = cache_ckv.shape[1]
    wts = _layout_weights(attn_norm[0], w_in[0], q_a_norm[0], w_q_b[0], kv_a_norm[0], w_k_b[0],
                          w_pool[0], pool_scale[0])
    wvb = jnp.transpose(w_v_b[0], (1, 0, 2)).astype(BF16)
    wout = w_out[0].astype(BF16)
    fn = final_norm.reshape(1, D_MODEL)

    mk, mv, mkb, mvb = _mem_kv(mem_prompt, mem_norm[0].reshape(1, D_MODEL), w_mem_kv[0].astype(BF16))
    cs_p, sn_p = _rope_tables(jnp.arange(SEQ))
    q_p, ckv_p, kr_p, kc_p, pool_p, cross_p, sg_p, pst_p = _proj_prompt(
        x_prompt, cs_p, sn_p, wts, mkb, mvb)
    mla_p = _attn_prompt(q_p, kc_p, wvb)
    rows_p = BATCH * SEQ
    y_p = _combine(x_prompt.reshape(rows_p, D_MODEL), mla_p.reshape(rows_p, MLA_WIDTH),
                   pool_p.reshape(rows_p, POOL_WIDTH), cross_p.reshape(rows_p, MEM_WIDTH),
                   sg_p.reshape(rows_p, D_MIX), wout, fn)

    rows_s = DEC_BATCH * DEC_SEQ
    cs_s, sn_s = _rope_tables(PAST_LEN + jnp.arange(DEC_SEQ))
    xs = x_sample.reshape(rows_s, D_MODEL)
    q_s, ckv_s, kr_s, kc_s, pool_s, qm_s, sg_s, pst_s = _proj_sample(
        xs, cs_s, sn_s, wts, state_pool[0])
    cross_s = _cross_sample(qm_s,
                            cache_mem_k[0].reshape(DEC_BATCH, MEM_TOKENS, MEM_WIDTH),
                            cache_mem_v[0].reshape(DEC_BATCH, MEM_TOKENS, MEM_WIDTH))
    o_lat_s = _attn_sample(page_table, q_s, kc_s,
                           cache_ckv.reshape(n_phys, PAGE_SIZE, KV_RANK),
                           cache_krope.reshape(n_phys, PAGE_SIZE, QK_ROPE_DIM))
    mla_s = _vup_sample(o_lat_s, wvb)
    y_s = _combine(xs, mla_s, pool_s, cross_s, sg_s, wout, fn)

    return (y_p.reshape(BATCH, SEQ, D_MODEL),
            y_s.reshape(DEC_BATCH, DEC_SEQ, D_MODEL),
            ckv_p[None],
            kr_p[None],
            pst_p[None],
            mk.reshape(1, BATCH, MEM_TOKENS, MEM_HEADS, MEM_HEAD_DIM),
            mv.reshape(1, BATCH, MEM_TOKENS, MEM_HEADS, MEM_HEAD_DIM),
            ckv_s.reshape(1, DEC_BATCH, DEC_SEQ, KV_RANK),
            kr_s.reshape(1, DEC_BATCH, DEC_SEQ, QK_ROPE_DIM),
            pst_s[None])
```

```python
import functools

import jax
import jax.numpy as jnp
from jax import lax
from jax.experimental import pallas as pl
from jax.experimental.pallas import tpu as pltpu

F32 = jnp.float32
BF16 = jnp.bfloat16

D_MODEL = 2048
BATCH = 2
SEQ = 4096
DEPTH = 1
DEC_BATCH = 128
DEC_SEQ = 8
PAST_LEN = 16384
PAGE_SIZE = 128
N_PAGES = PAST_LEN // PAGE_SIZE

N_HEADS = 8
QK_NOPE_DIM = 128
QK_ROPE_DIM = 64
V_HEAD_DIM = 128
Q_RANK = 512
KV_RANK = 256
MLA_WIDTH = N_HEADS * V_HEAD_DIM
MLA_SCALE = (QK_NOPE_DIM + QK_ROPE_DIM) ** -0.5
ROPE_THETA = 10000.0
POOL_GROUPS = 4
POOL_WINDOWS = (2, 4, 8, 16)
POOL_GROUP_DIM = 128
POOL_WIDTH = POOL_GROUPS * POOL_GROUP_DIM
POOL_STATE_LEN = 15
MEM_TOKENS = 256
MEM_HEADS = 4
MEM_HEAD_DIM = 128
MEM_WIDTH = MEM_HEADS * MEM_HEAD_DIM
MEM_SCALE = MEM_HEAD_DIM ** -0.5
D_MIX = MLA_WIDTH + POOL_WIDTH + MEM_WIDTH
RMS_EPS = 1e-6

LANE = 128
ROPE_PAD = LANE
QK_WIDTH = KV_RANK + ROPE_PAD
C_QA = 0
C_CKV = C_QA + Q_RANK
C_KR = C_CKV + KV_RANK
C_U = C_KR + 2 * ROPE_PAD
C_QM = C_U + POOL_WIDTH
C_GATE = C_QM + MEM_WIDTH
C_END = C_GATE + D_MIX
Q_NOPE_W = N_HEADS * QK_NOPE_DIM
Q_ROPE_W = N_HEADS * ROPE_PAD

VMEM_LIMIT = 56 * 1024 * 1024
NEG_BIG = -1e30

PROJ_ROWS = 256
ATT_Q = 128
ATT_K = 512
ATT_SUB_HEADS = 2
PAGES_PER_STEP = 32
CROSS_SEQS = 4


def _const_spec(shape):
    n = len(shape)
    return pl.BlockSpec(shape, lambda *_: (0,) * n, pipeline_mode=pl.Buffered(1))


def _params(n_axes):
    return pltpu.CompilerParams(dimension_semantics=("arbitrary",) * n_axes,
                                vmem_limit_bytes=VMEM_LIMIT)


def _rms(x, g):
    return x * lax.rsqrt(jnp.mean(x * x, axis=-1, keepdims=True) + RMS_EPS) * g


def _dot(a, b):
    return jnp.dot(a, b, preferred_element_type=F32)


def _dot_nt(a, b):
    return lax.dot_general(a, b, (((1,), (1,)), ((), ())), preferred_element_type=F32)


def _softmax_rows(s):
    m = jnp.max(s, axis=-1, keepdims=True)
    e = jnp.exp(s - m)
    return e / jnp.sum(e, axis=-1, keepdims=True)


def _mem_kv_kernel(mem_ref, g_ref, w_ref, k_ref, v_ref, kb_ref, vb_ref):
    h = _rms(mem_ref[...], g_ref[...]).astype(BF16)
    kv = _dot(h, w_ref[...])
    k = kv[:, :MEM_WIDTH]
    v = kv[:, MEM_WIDTH:]
    k_ref[...] = k
    v_ref[...] = v
    kb_ref[...] = k.astype(BF16)
    vb_ref[...] = v.astype(BF16)


def _mem_kv(mem, mem_norm, w_mem_kv):
    b = mem.shape[0]
    out_f32 = jax.ShapeDtypeStruct((b, MEM_TOKENS, MEM_WIDTH), F32)
    out_bf16 = jax.ShapeDtypeStruct((b, MEM_TOKENS, MEM_WIDTH), BF16)
    blk = pl.BlockSpec((None, MEM_TOKENS, MEM_WIDTH), lambda i: (i, 0, 0))
    return pl.pallas_call(
        _mem_kv_kernel,
        grid=(b,),
        in_specs=[pl.BlockSpec((None, MEM_TOKENS, D_MODEL), lambda i: (i, 0, 0)),
                  _const_spec((1, D_MODEL)),
                  _const_spec((D_MODEL, 2 * MEM_WIDTH))],
        out_specs=[blk, blk, blk, blk],
        out_shape=[out_f32, out_f32, out_bf16, out_bf16],
        compiler_params=_params(1),
        name="mem_kv",
    )(mem, mem_norm, w_mem_kv)


def _project(x, cs, sn, an_ref, w1_ref, qan_ref, wq_ref, kvn_ref, wkb_ref):
    h = _rms(x, an_ref[...]).astype(BF16)

    def seg(a, b):
        return _dot(h, w1_ref[:, a:b])

    qa = _rms(seg(C_QA, C_CKV), qan_ref[...]).astype(BF16)
    q_nope = _dot(qa, wq_ref[:, 0:Q_NOPE_W])
    q_rope = _dot(qa, wq_ref[:, Q_NOPE_W:Q_NOPE_W + Q_ROPE_W])
    q_swap = _dot(qa, wq_ref[:, Q_NOPE_W + Q_ROPE_W:Q_NOPE_W + 2 * Q_ROPE_W])
    heads = []
    for hh in range(N_HEADS):
        sl = slice(hh * LANE, (hh + 1) * LANE)
        q_lat = _dot(q_nope[:, sl].astype(BF16), wkb_ref[hh]) * MLA_SCALE
        q_r = (q_rope[:, sl] * cs + q_swap[:, sl] * sn) * MLA_SCALE
        heads.append((q_lat, q_r))
    ckv = _rms(seg(C_CKV, C_KR), kvn_ref[...])
    zk = seg(C_KR, C_U)
    kr = zk[:, :ROPE_PAD] * cs + zk[:, ROPE_PAD:] * sn
    u = seg(C_U, C_QM)
    qm = seg(C_QM, C_GATE)
    g = seg(C_GATE, C_END)
    sg = g / (1.0 + jnp.exp(-g))
    return heads, ckv, kr, u, qm, sg


def _proj_prompt_kernel(x_ref, cs_ref, sn_ref, an_ref, w1_ref, qan_ref, wq_ref, kvn_ref, wkb_ref,
                        wpool_ref, pscale_ref, mk_ref, mv_ref,
                        q_ref, ckv_ref, kr_ref, kc_ref, pool_ref, cross_ref, sg_ref, pst_ref,
                        uext_ref):
    j = pl.program_id(1)
    tm = PROJ_ROWS
    halo = 16
    heads, ckv, kr, u, qm, sg = _project(
        x_ref[...], cs_ref[...], sn_ref[...], an_ref, w1_ref, qan_ref, wq_ref, kvn_ref, wkb_ref)

    for hh, (q_lat, q_r) in enumerate(heads):
        q_ref[hh, :, 0:KV_RANK] = q_lat.astype(BF16)
        q_ref[hh, :, KV_RANK:QK_WIDTH] = q_r.astype(BF16)
    ckv_ref[...] = ckv
    kr_ref[...] = kr[:, :QK_ROPE_DIM]
    kc_ref[:, 0:KV_RANK] = ckv.astype(BF16)
    kc_ref[:, KV_RANK:QK_WIDTH] = kr.astype(BF16)
    sg_ref[...] = sg.astype(BF16)

    @pl.when(j == 0)
    def _():
        uext_ref[0:halo, :] = jnp.zeros((halo, POOL_WIDTH), F32)

    uext_ref[halo:halo + tm, :] = u
    pos = j * tm + lax.broadcasted_iota(jnp.int32, (tm, 1), 0)
    for g, w in enumerate(POOL_WINDOWS):
        sl = slice(g * POOL_GROUP_DIM, (g + 1) * POOL_GROUP_DIM)
        acc = u[:, sl]
        for k in range(1, w):
            acc = acc + uext_ref[halo - k:halo - k + tm, sl]
        cnt = jnp.minimum(pos + 1, w).astype(F32)
        diff = acc / cnt - u[:, sl]
        y = _dot(diff.astype(BF16), wpool_ref[g]) * pscale_ref[:, sl]
        pool_ref[:, sl] = y.astype(BF16)

    @pl.when(j == pl.num_programs(1) - 1)
    def _():
        pst_ref[...] = uext_ref[halo + tm - POOL_STATE_LEN:halo + tm, :]

    uext_ref[0:halo, :] = uext_ref[tm:tm + halo, :]

    for hh in range(MEM_HEADS):
        sl = slice(hh * MEM_HEAD_DIM, (hh + 1) * MEM_HEAD_DIM)
        s = _dot_nt(qm[:, sl].astype(BF16), mk_ref[:, sl]) * MEM_SCALE
        p = _softmax_rows(s).astype(BF16)
        cross_ref[:, sl] = _dot(p, mv_ref[:, sl]).astype(BF16)


def _proj_prompt(x, cs, sn, wts, mkb, mvb):
    b, t, _ = x.shape
    tm = PROJ_ROWS
    nj = t // tm

    def rows(width, dtype):
        return (pl.BlockSpec((None, tm, width), lambda i, j: (i, j, 0)),
                jax.ShapeDtypeStruct((b, t, width), dtype))

    outs = [
        (pl.BlockSpec((None, N_HEADS, tm, QK_WIDTH), lambda i, j: (i, 0, j, 0)),
         jax.ShapeDtypeStruct((b, N_HEADS, t, QK_WIDTH), BF16)),
        rows(KV_RANK, F32),
        rows(QK_ROPE_DIM, F32),
        rows(QK_WIDTH, BF16),
        rows(POOL_WIDTH, BF16),
        rows(MEM_WIDTH, BF16),
        rows(D_MIX, BF16),
        (pl.BlockSpec((None, POOL_STATE_LEN, POOL_WIDTH), lambda i, j: (i, 0, 0)),
         jax.ShapeDtypeStruct((b, POOL_STATE_LEN, POOL_WIDTH), F32)),
    ]
    mem_spec = pl.BlockSpec((None, MEM_TOKENS, MEM_WIDTH), lambda i, j: (i, 0, 0))
    in_specs = [
        pl.BlockSpec((None, tm, D_MODEL), lambda i, j: (i, j, 0)),
        pl.BlockSpec((tm, ROPE_PAD), lambda i, j: (j, 0)),
        pl.BlockSpec((tm, ROPE_PAD), lambda i, j: (j, 0)),
    ] + [_const_spec(w.shape) for w in wts] + [mem_spec, mem_spec]
    return pl.pallas_call(
        _proj_prompt_kernel,
        grid=(b, nj),
        in_specs=in_specs,
        out_specs=[o[0] for o in outs],
        out_shape=[o[1] for o in outs],
        scratch_shapes=[pltpu.VMEM((16 + tm, POOL_WIDTH), F32)],
        compiler_params=_params(2),
        name="proj_prompt",
    )(x, cs, sn, *wts, mkb, mvb)


def _attn_prompt_kernel(q_ref, kc_ref, wvb_ref, o_ref, s_even, s_odd, m_ref, l_ref, acc_ref):
    i = pl.program_id(1)
    rows = N_HEADS * ATT_Q
    sub = ATT_SUB_HEADS * ATT_Q
    n_sub = N_HEADS // ATT_SUB_HEADS
    m_ref[...] = jnp.full((rows, 1), NEG_BIG, F32)
    l_ref[...] = jnp.zeros((rows, 1), F32)
    acc_ref[...] = jnp.zeros((rows, KV_RANK), F32)

    def keys(c):
        return kc_ref[pl.ds(pl.multiple_of(c * ATT_K, ATT_K), ATT_K), :]

    def scores(c, s_ref):
        k = keys(c)
        for sb in range(n_sub):
            q = q_ref[sb * ATT_SUB_HEADS:(sb + 1) * ATT_SUB_HEADS].reshape(sub, QK_WIDTH)
            s_ref[sb * sub:(sb + 1) * sub, :] = _dot_nt(q, k)

    def attend(c, s_ref, masked):
        v = keys(c)[:, 0:KV_RANK]
        if masked:
            key_pos = c * ATT_K + lax.broadcasted_iota(jnp.int32, (ATT_Q, ATT_K), 1)
            q_pos = i * ATT_Q + lax.broadcasted_iota(jnp.int32, (ATT_Q, ATT_K), 0)
            visible = (key_pos <= q_pos)[None]
        for sb in range(n_sub):
            r = slice(sb * sub, (sb + 1) * sub)
            s = s_ref[r, :]
            if masked:
                s = jnp.where(visible, s.reshape(ATT_SUB_HEADS, ATT_Q, ATT_K), NEG_BIG)
                s = s.reshape(sub, ATT_K)
            m_prev = m_ref[r, :]
            m_new = jnp.maximum(m_prev, jnp.max(s, axis=-1, keepdims=True))
            alpha = jnp.exp(m_prev - m_new)
            p = jnp.exp(s - m_new)
            l_ref[r, :] = alpha * l_ref[r, :] + jnp.sum(p, axis=-1, keepdims=True)
            acc_ref[r, :] = alpha * acc_ref[r, :] + _dot(p.astype(BF16), v)
            m_ref[r, :] = m_new

    n_full = (i * ATT_Q) // ATT_K
    scores(0, s_even)

    def body(c, carry):
        @pl.when(lax.rem(c, 2) == 0)
        def _():
            scores(c + 1, s_odd)
            attend(c, s_even, masked=False)

        @pl.when(lax.rem(c, 2) == 1)
        def _():
            scores(c + 1, s_even)
            attend(c, s_odd, masked=False)

        return carry

    lax.fori_loop(0, n_full, body, 0)

    @pl.when(lax.rem(n_full, 2) == 0)
    def _():
        attend(n_full, s_even, masked=True)

    @pl.when(lax.rem(n_full, 2) == 1)
    def _():
        attend(n_full, s_odd, masked=True)

    for hh in range(N_HEADS):
        r = slice(hh * ATT_Q, (hh + 1) * ATT_Q)
        o_h = (acc_ref[r, :] / l_ref[r, :]).astype(BF16)
        o_ref[:, hh * V_HEAD_DIM:(hh + 1) * V_HEAD_DIM] = _dot(o_h, wvb_ref[hh]).astype(BF16)


def _attn_prompt(q, kc, wvb):
    b, _, t, _ = q.shape
    rows = N_HEADS * ATT_Q
    return pl.pallas_call(
        _attn_prompt_kernel,
        grid=(b, t // ATT_Q),
        in_specs=[pl.BlockSpec((None, N_HEADS, ATT_Q, QK_WIDTH), lambda bi, i: (bi, 0, i, 0)),
                  pl.BlockSpec((None, t, QK_WIDTH), lambda bi, i: (bi, 0, 0)),
                  _const_spec(wvb.shape)],
        out_specs=pl.BlockSpec((None, ATT_Q, MLA_WIDTH), lambda bi, i: (bi, i, 0)),
        out_shape=jax.ShapeDtypeStruct((b, t, MLA_WIDTH), BF16),
        scratch_shapes=[pltpu.VMEM((rows, ATT_K), F32), pltpu.VMEM((rows, ATT_K), F32),
                        pltpu.VMEM((rows, 1), F32), pltpu.VMEM((rows, 1), F32),
                        pltpu.VMEM((rows, KV_RANK), F32)],
        compiler_params=_params(2),
        name="attn_prompt",
    )(q, kc, wvb)


def _combine_kernel(x_ref, mla_ref, pool_ref, cross_ref, sg_ref, wout_ref, fn_ref, y_ref):
    def part(ref, a, b):
        mix = ref[...].astype(F32) * sg_ref[:, a:b].astype(F32)
        return _dot(mix.astype(BF16), wout_ref[a:b, :])

    acc = x_ref[...] + part(mla_ref, 0, MLA_WIDTH)
    acc = acc + part(pool_ref, MLA_WIDTH, MLA_WIDTH + POOL_WIDTH)
    acc = acc + part(cross_ref, MLA_WIDTH + POOL_WIDTH, D_MIX)
    y_ref[...] = _rms(acc, fn_ref[...])


def _combine(x, mla, pool, cross, sg, wout, fn):
    n = x.shape[0]
    tm = PROJ_ROWS

    def rows(width):
        return pl.BlockSpec((tm, width), lambda i: (i, 0))

    return pl.pallas_call(
        _combine_kernel,
        grid=(n // tm,),
        in_specs=[rows(D_MODEL), rows(MLA_WIDTH), rows(POOL_WIDTH), rows(MEM_WIDTH), rows(D_MIX),
                  _const_spec(wout.shape), _const_spec(fn.shape)],
        out_specs=rows(D_MODEL),
        out_shape=jax.ShapeDtypeStruct((n, D_MODEL), F32),
        compiler_params=_params(1),
        name="combine",
    )(x, mla, pool, cross, sg, wout, fn)


def _proj_sample_kernel(x_ref, cs_ref, sn_ref, an_ref, w1_ref, qan_ref, wq_ref, kvn_ref, wkb_ref,
                        wpool_ref, pscale_ref, st_ref,
                        q_ref, ckv_ref, kr_ref, kc_ref, pool_ref, qm_ref, sg_ref, pst_ref,
                        ext_ref):
    tm = PROJ_ROWS
    ns = tm // DEC_SEQ
    cs = jnp.broadcast_to(cs_ref[...][None], (ns, DEC_SEQ, ROPE_PAD)).reshape(tm, ROPE_PAD)
    sn = jnp.broadcast_to(sn_ref[...][None], (ns, DEC_SEQ, ROPE_PAD)).reshape(tm, ROPE_PAD)
    heads, ckv, kr, u, qm, sg = _project(
        x_ref[...], cs, sn, an_ref, w1_ref, qan_ref, wq_ref, kvn_ref, wkb_ref)

    for hh, (q_lat, q_r) in enumerate(heads):
        q_ref[:, hh, :, 0:KV_RANK] = q_lat.reshape(ns, DEC_SEQ, KV_RANK)
        q_ref[:, hh, :, KV_RANK:QK_WIDTH] = q_r.reshape(ns, DEC_SEQ, ROPE_PAD)
    ckv_ref[...] = ckv
    kr_ref[...] = kr[:, :QK_ROPE_DIM]
    kc_ref[:, 0:KV_RANK] = ckv
    kc_ref[:, KV_RANK:QK_WIDTH] = kr
    qm_ref[...] = qm
    sg_ref[...] = sg.astype(BF16)

    base = POOL_STATE_LEN + 1
    ext_ref[:, 1:base, :] = st_ref[...]
    ext_ref[:, base:base + DEC_SEQ, :] = u.reshape(ns, DEC_SEQ, POOL_WIDTH)
    for g, w in enumerate(POOL_WINDOWS):
        sl = slice(g * POOL_GROUP_DIM, (g + 1) * POOL_GROUP_DIM)
        acc = ext_ref[:, base:base + DEC_SEQ, sl]
        for k in range(1, w):
            acc = acc + ext_ref[:, base - k:base - k + DEC_SEQ, sl]
        diff = (acc * (1.0 / w)).reshape(tm, POOL_GROUP_DIM) - u[:, sl]
        y = _dot(diff.astype(BF16), wpool_ref[g]) * pscale_ref[:, sl]
        pool_ref[:, sl] = y.astype(BF16)
    pst_ref[...] = ext_ref[:, base + DEC_SEQ - POOL_STATE_LEN:base + DEC_SEQ, :]


def _proj_sample(x, cs, sn, wts, state):
    n = x.shape[0]
    tm = PROJ_ROWS
    ns = tm // DEC_SEQ
    nseq = n // DEC_SEQ

    def rows(width, dtype):
        return (pl.BlockSpec((tm, width), lambda i: (i, 0)), jax.ShapeDtypeStruct((n, width), dtype))

    outs = [
        (pl.BlockSpec((ns, N_HEADS, DEC_SEQ, QK_WIDTH), lambda i: (i, 0, 0, 0)),
         jax.ShapeDtypeStruct((nseq, N_HEADS, DEC_SEQ, QK_WIDTH), F32)),
        rows(KV_RANK, F32),
        rows(QK_ROPE_DIM, F32),
        rows(QK_WIDTH, F32),
        rows(POOL_WIDTH, BF16),
        rows(MEM_WIDTH, F32),
        rows(D_MIX, BF16),
        (pl.BlockSpec((ns, POOL_STATE_LEN, POOL_WIDTH), lambda i: (i, 0, 0)),
         jax.ShapeDtypeStruct((nseq, POOL_STATE_LEN, POOL_WIDTH), F32)),
    ]
    in_specs = [
        pl.BlockSpec((tm, D_MODEL), lambda i: (i, 0)),
        _const_spec(cs.shape),
        _const_spec(sn.shape),
    ] + [_const_spec(w.shape) for w in wts] + [
        pl.BlockSpec((ns, POOL_STATE_LEN, POOL_WIDTH), lambda i: (i, 0, 0))]
    return pl.pallas_call(
        _proj_sample_kernel,
        grid=(n // tm,),
        in_specs=in_specs,
        out_specs=[o[0] for o in outs],
        out_shape=[o[1] for o in outs],
        scratch_shapes=[pltpu.VMEM((ns, POOL_STATE_LEN + 1 + DEC_SEQ, POOL_WIDTH), F32)],
        compiler_params=_params(1),
        name="proj_sample",
    )(x, cs, sn, *wts, state)


def _cross_sample_kernel(qm_ref, k_ref, v_ref, o_ref):
    rows = MEM_HEADS * DEC_SEQ
    r_head = lax.broadcasted_iota(jnp.int32, (rows, MEM_WIDTH), 0) // DEC_SEQ
    c_head = lax.broadcasted_iota(jnp.int32, (rows, MEM_WIDTH), 1) // MEM_HEAD_DIM
    head_mask = r_head == c_head
    outs = []
    for s_i in range(CROSS_SEQS):
        q = qm_ref[s_i * DEC_SEQ:(s_i + 1) * DEC_SEQ, :]
        q_bd = jnp.where(head_mask, jnp.concatenate([q] * MEM_HEADS, axis=0), 0.0).astype(BF16)
        k = jnp.concatenate([k_ref[s_i, :, hh, :] for hh in range(MEM_HEADS)], axis=-1).astype(BF16)
        v = jnp.concatenate([v_ref[s_i, :, hh, :] for hh in range(MEM_HEADS)], axis=-1).astype(BF16)
        p = _softmax_rows(_dot_nt(q_bd, k) * MEM_SCALE).astype(BF16)
        o_bd = jnp.where(head_mask, _dot(p, v), 0.0)
        o = o_bd[0:DEC_SEQ]
        for hh in range(1, MEM_HEADS):
            o = o + o_bd[hh * DEC_SEQ:(hh + 1) * DEC_SEQ]
        outs.append(o)
    o_ref[...] = jnp.concatenate(outs, axis=0).astype(BF16)


def _cross_sample(qm, mem_k, mem_v):
    n = qm.shape[0]
    tm = CROSS_SEQS * DEC_SEQ
    kv_spec = pl.BlockSpec((CROSS_SEQS, MEM_TOKENS, MEM_HEADS, MEM_HEAD_DIM), lambda i: (i, 0, 0, 0))
    return pl.pallas_call(
        _cross_sample_kernel,
        grid=(n // tm,),
        in_specs=[pl.BlockSpec((tm, MEM_WIDTH), lambda i: (i, 0)), kv_spec, kv_spec],
        out_specs=pl.BlockSpec((tm, MEM_WIDTH), lambda i: (i, 0)),
        out_shape=jax.ShapeDtypeStruct((n, MEM_WIDTH), BF16),
        compiler_params=_params(1),
        name="cross_sample",
    )(qm, mem_k, mem_v)


def _page_copies(pages, ckv_hbm, krt_hbm, ckv_buf, krt_buf, sem, slot):
    copies = []
    for j, pg in enumerate(pages):
        copies.append(pltpu.make_async_copy(
            ckv_hbm.at[pg], ckv_buf.at[slot, pl.ds(j * PAGE_SIZE, PAGE_SIZE), :], sem.at[slot]))
        copies.append(pltpu.make_async_copy(krt_hbm.at[pg], krt_buf.at[slot, j], sem.at[slot]))
    return copies


def _attn_sample_kernel(pt_ref, q_ref, knew_ref, ckv_hbm, krt_hbm, o_ref,
                        ckv_buf, krt_buf, sem, m_ref, l_ref, acc_ref):
    npg = PAGES_PER_STEP
    s_i = pl.program_id(0)
    c = pl.program_id(1)
    nch = pl.num_programs(1)
    step = s_i * nch + c
    last_step = pl.num_programs(0) * nch - 1
    slot = lax.rem(step, 2)
    rows = N_HEADS * DEC_SEQ

    def copies(seq, chunk, to_slot):
        pages = [pt_ref[seq, chunk * npg + j] for j in range(npg)]
        return _page_copies(pages, ckv_hbm, krt_hbm, ckv_buf, krt_buf, sem, to_slot)

    @pl.when(step == 0)
    def _():
        for cp in copies(0, 0, 0):
            cp.start()

    @pl.when(step < last_step)
    def _():
        wrap = c + 1 == nch
        for cp in copies(jnp.where(wrap, s_i + 1, s_i), jnp.where(wrap, 0, c + 1), 1 - slot):
            cp.start()

    for cp in _page_copies([0] * npg, ckv_hbm, krt_hbm, ckv_buf, krt_buf, sem, slot):
        cp.wait()

    q = q_ref[...].reshape(rows, QK_WIDTH).astype(BF16)

    @pl.when(c == 0)
    def _():
        pad = jnp.zeros((PAGE_SIZE - DEC_SEQ, QK_WIDTH), F32)
        kn = jnp.concatenate([knew_ref[...], pad], axis=0).astype(BF16)
        s = _dot_nt(q, kn)
        key_j = lax.broadcasted_iota(jnp.int32, (rows, PAGE_SIZE), 1)
        q_t = lax.broadcasted_iota(jnp.int32, (rows, PAGE_SIZE), 0) % DEC_SEQ
        s = jnp.where(key_j <= q_t, s, NEG_BIG)
        m = jnp.max(s, axis=-1, keepdims=True)
        p = jnp.exp(s - m)
        m_ref[...] = m
        l_ref[...] = jnp.sum(p, axis=-1, keepdims=True)
        acc_ref[...] = _dot(p.astype(BF16), kn[:, 0:KV_RANK])

    kc = ckv_buf[slot].astype(BF16)
    krt = jnp.concatenate([krt_buf[slot, j] for j in range(npg)], axis=-1).astype(BF16)
    s = _dot_nt(q[:, 0:KV_RANK], kc) + _dot(q[:, KV_RANK:KV_RANK + QK_ROPE_DIM], krt)
    m_prev = m_ref[...]
    m_new = jnp.maximum(m_prev, jnp.max(s, axis=-1, keepdims=True))
    alpha = jnp.exp(m_prev - m_new)
    p = jnp.exp(s - m_new)
    l_new = alpha * l_ref[...] + jnp.sum(p, axis=-1, keepdims=True)
    acc_new = alpha * acc_ref[...] + _dot(p.astype(BF16), kc)
    m_ref[...] = m_new
    l_ref[...] = l_new
    acc_ref[...] = acc_new

    @pl.when(c == nch - 1)
    def _():
        o_ref[...] = (acc_new / l_new).reshape(N_HEADS, DEC_SEQ, KV_RANK)


def _attn_sample(page_table, q, knew, cache_ckv, cache_krope_t):
    nseq = q.shape[0]
    npg = PAGES_PER_STEP
    rows = N_HEADS * DEC_SEQ
    grid_spec = pltpu.PrefetchScalarGridSpec(
        num_scalar_prefetch=1,
        grid=(nseq, N_PAGES // npg),
        in_specs=[pl.BlockSpec((None, N_HEADS, DEC_SEQ, QK_WIDTH), lambda s, c, pt: (s, 0, 0, 0)),
                  pl.BlockSpec((DEC_SEQ, QK_WIDTH), lambda s, c, pt: (s, 0)),
                  pl.BlockSpec(memory_space=pl.ANY),
                  pl.BlockSpec(memory_space=pl.ANY)],
        out_specs=pl.BlockSpec((None, N_HEADS, DEC_SEQ, KV_RANK), lambda s, c, pt: (s, 0, 0, 0)),
        scratch_shapes=[pltpu.VMEM((2, npg * PAGE_SIZE, KV_RANK), F32),
                        pltpu.VMEM((2, npg, QK_ROPE_DIM, PAGE_SIZE), F32),
                        pltpu.SemaphoreType.DMA((2,)),
                        pltpu.VMEM((rows, 1), F32), pltpu.VMEM((rows, 1), F32),
                        pltpu.VMEM((rows, KV_RANK), F32)],
    )
    return pl.pallas_call(
        _attn_sample_kernel,
        grid_spec=grid_spec,
        out_shape=jax.ShapeDtypeStruct((nseq, N_HEADS, DEC_SEQ, KV_RANK), F32),
        compiler_params=_params(2),
        name="attn_sample",
    )(page_table, q, knew, cache_ckv, cache_krope_t)


def _vup_sample_kernel(o_ref, wvb_ref, mla_ref):
    ns = PROJ_ROWS // DEC_SEQ
    for hh in range(N_HEADS):
        o_h = o_ref[:, hh, :, :].reshape(ns * DEC_SEQ, KV_RANK).astype(BF16)
        mla_ref[:, hh * V_HEAD_DIM:(hh + 1) * V_HEAD_DIM] = _dot(o_h, wvb_ref[hh]).astype(BF16)


def _vup_sample(o_lat, wvb):
    nseq = o_lat.shape[0]
    ns = PROJ_ROWS // DEC_SEQ
    return pl.pallas_call(
        _vup_sample_kernel,
        grid=(nseq // ns,),
        in_specs=[pl.BlockSpec((ns, N_HEADS, DEC_SEQ, KV_RANK), lambda i: (i, 0, 0, 0)),
                  _const_spec(wvb.shape)],
        out_specs=pl.BlockSpec((PROJ_ROWS, MLA_WIDTH), lambda i: (i, 0)),
        out_shape=jax.ShapeDtypeStruct((nseq * DEC_SEQ, MLA_WIDTH), BF16),
        compiler_params=_params(1),
        name="vup_sample",
    )(o_lat, wvb)


def _rope_tables(pos):
    inv = ROPE_THETA ** (-jnp.arange(0, QK_ROPE_DIM, 2, dtype=F32) / QK_ROPE_DIM)
    ang = pos.astype(F32)[:, None] * inv[None, :]
    cos, sin = jnp.cos(ang), jnp.sin(ang)
    reps = ROPE_PAD // QK_ROPE_DIM
    cs = jnp.tile(jnp.concatenate([cos, cos], axis=-1), (1, reps))
    sn = jnp.tile(jnp.concatenate([-sin, sin], axis=-1), (1, reps))
    return cs, sn


def _swap_halves(w):
    half = w.shape[-1] // 2
    return jnp.concatenate([w[..., half:], w[..., :half]], axis=-1)


def _pad_lanes(w, width):
    return jnp.pad(w, [(0, 0)] * (w.ndim - 1) + [(0, width - w.shape[-1])])


def _layout_weights(attn_norm, w_in, q_a_norm, w_q_b, kv_a_norm, w_k_b, w_pool, pool_scale):
    s0 = Q_RANK
    s1 = s0 + KV_RANK
    s2 = s1 + QK_ROPE_DIM
    w_kr = w_in[:, s1:s2]
    w1 = jnp.concatenate([
        w_in[:, :s1],
        _pad_lanes(w_kr, ROPE_PAD), _pad_lanes(_swap_halves(w_kr), ROPE_PAD),
        w_in[:, s2:]], axis=1).astype(BF16)
    wq_nope = w_q_b[:, :, :QK_NOPE_DIM].reshape(Q_RANK, Q_NOPE_W)
    wq_r = w_q_b[:, :, QK_NOPE_DIM:]
    wq_rope = _pad_lanes(wq_r, ROPE_PAD).reshape(Q_RANK, Q_ROPE_W)
    wq_swap = _pad_lanes(_swap_halves(wq_r), ROPE_PAD).reshape(Q_RANK, Q_ROPE_W)
    wq = jnp.concatenate([wq_nope, wq_rope, wq_swap], axis=1).astype(BF16)
    wkb = jnp.transpose(w_k_b, (1, 2, 0)).astype(BF16)
    return (attn_norm.reshape(1, D_MODEL), w1, q_a_norm.reshape(1, Q_RANK), wq,
            kv_a_norm.reshape(1, KV_RANK), wkb, w_pool.astype(BF16),
            pool_scale.reshape(1, POOL_WIDTH))


def kernel(x_prompt, x_sample, mem_prompt, cache_ckv, cache_krope, page_table, state_pool,
           cache_mem_k, cache_mem_v, attn_norm, mem_norm, w_in, q_a_norm, w_q_b, kv_a_norm,
           w_k_b, w_v_b, w_pool, pool_scale, w_mem_kv, w_out, final_norm):
    assert DEPTH == 1 and attn_norm.shape[0] == 1
    n_phys = cache_ckv.shape[1]
    wts = _layout_weights(attn_norm[0], w_in[0], q_a_norm[0], w_q_b[0], kv_a_norm[0], w_k_b[0],
                          w_pool[0], pool_scale[0])
    wvb = jnp.transpose(w_v_b[0], (1, 0, 2)).astype(BF16)
    wout = w_out[0].astype(BF16)
    fn = final_norm.reshape(1, D_MODEL)

    mk, mv, mkb, mvb = _mem_kv(mem_prompt, mem_norm[0].reshape(1, D_MODEL), w_mem_kv[0].astype(BF16))
    cs_p, sn_p = _rope_tables(jnp.arange(SEQ))
    q_p, ckv_p, kr_p, kc_p, pool_p, cross_p, sg_p, pst_p = _proj_prompt(
        x_prompt, cs_p, sn_p, wts, mkb, mvb)
    mla_p = _attn_prompt(q_p, kc_p, wvb)
    rows_p = BATCH * SEQ
    y_p = _combine(x_prompt.reshape(rows_p, D_MODEL), mla_p.reshape(rows_p, MLA_WIDTH),
                   pool_p.reshape(rows_p, POOL_WIDTH), cross_p.reshape(rows_p, MEM_WIDTH),
                   sg_p.reshape(rows_p, D_MIX), wout, fn)

    rows_s = DEC_BATCH * DEC_SEQ
    cs_s, sn_s = _rope_tables(PAST_LEN + jnp.arange(DEC_SEQ))
    xs = x_sample.reshape(rows_s, D_MODEL)
    q_s, ckv_s, kr_s, kc_s, pool_s, qm_s, sg_s, pst_s = _proj_sample(
        xs, cs_s, sn_s, wts, state_pool[0])
    cross_s = _cross_sample(qm_s, cache_mem_k[0], cache_mem_v[0])
    o_lat_s = _attn_sample(page_table, q_s, kc_s,
                           cache_ckv.reshape(n_phys, PAGE_SIZE, KV_RANK),
                           jnp.swapaxes(cache_krope.reshape(n_phys, PAGE_SIZE, QK_ROPE_DIM), 1, 2))
    mla_s = _vup_sample(o_lat_s, wvb)
    y_s = _combine(xs, mla_s, pool_s, cross_s, sg_s, wout, fn)

    return (y_p.reshape(BATCH, SEQ, D_MODEL),
            y_s.reshape(DEC_BATCH, DEC_SEQ, D_MODEL),
            ckv_p[None],
            kr_p[None],
            pst_p[None],
            mk.reshape(1, BATCH, MEM_TOKENS, MEM_HEADS, MEM_HEAD_DIM),
            mv.reshape(1, BATCH, MEM_TOKENS, MEM_HEADS, MEM_HEAD_DIM),
            ckv_s.reshape(1, DEC_BATCH, DEC_SEQ, KV_RANK),
            kr_s.reshape(1, DEC_BATCH, DEC_SEQ, QK_ROPE_DIM),
            pst_s[None])
```

```python
import functools

import jax
import jax.numpy as jnp
from jax import lax
from jax.experimental import pallas as pl
from jax.experimental.pallas import tpu as pltpu

F32 = jnp.float32
BF16 = jnp.bfloat16

D_MODEL = 2048
BATCH = 2
SEQ = 4096
DEPTH = 1
DEC_BATCH = 128
DEC_SEQ = 8
PAST_LEN = 16384
PAGE_SIZE = 128
N_PAGES = PAST_LEN // PAGE_SIZE

N_HEADS = 8
QK_NOPE_DIM = 128
QK_ROPE_DIM = 64
V_HEAD_DIM = 128
Q_RANK = 512
KV_RANK = 256
MLA_WIDTH = N_HEADS * V_HEAD_DIM
MLA_SCALE = (QK_NOPE_DIM + QK_ROPE_DIM) ** -0.5
ROPE_THETA = 10000.0
POOL_GROUPS = 4
POOL_WINDOWS = (2, 4, 8, 16)
POOL_GROUP_DIM = 128
POOL_WIDTH = POOL_GROUPS * POOL_GROUP_DIM
POOL_STATE_LEN = 15
MEM_TOKENS = 256
MEM_HEADS = 4
MEM_HEAD_DIM = 128
MEM_WIDTH = MEM_HEADS * MEM_HEAD_DIM
MEM_SCALE = MEM_HEAD_DIM ** -0.5
D_MIX = MLA_WIDTH + POOL_WIDTH + MEM_WIDTH
RMS_EPS = 1e-6

LANE = 128
ROPE_PAD = LANE
QK_WIDTH = KV_RANK + ROPE_PAD
C_QA = 0
C_CKV = C_QA + Q_RANK
C_KR = C_CKV + KV_RANK
C_U = C_KR + 2 * ROPE_PAD
C_QM = C_U + POOL_WIDTH
C_GATE = C_QM + MEM_WIDTH
C_END = C_GATE + D_MIX
Q_NOPE_W = N_HEADS * QK_NOPE_DIM
Q_ROPE_W = N_HEADS * ROPE_PAD

VMEM_LIMIT = 56 * 1024 * 1024
NEG_BIG = -1e30

PROJ_ROWS = 256
ATT_Q = 128
ATT_K = 512
ATT_SUB_HEADS = 2
PAGES_PER_STEP = 32
CROSS_SEQS = 4


def _const_spec(shape):
    n = len(shape)
    return pl.BlockSpec(shape, lambda *_: (0,) * n, pipeline_mode=pl.Buffered(1))


def _params(n_axes):
    return pltpu.CompilerParams(dimension_semantics=("arbitrary",) * n_axes,
                                vmem_limit_bytes=VMEM_LIMIT)


def _rms(x, g):
    return x * lax.rsqrt(jnp.mean(x * x, axis=-1, keepdims=True) + RMS_EPS) * g


def _dot(a, b):
    return jnp.dot(a, b, preferred_element_type=F32)


def _dot_nt(a, b):
    return lax.dot_general(a, b, (((1,), (1,)), ((), ())), preferred_element_type=F32)


def _softmax_rows(s):
    m = jnp.max(s, axis=-1, keepdims=True)
    e = jnp.exp(s - m)
    return e / jnp.sum(e, axis=-1, keepdims=True)


def _mem_kv_kernel(mem_ref, g_ref, w_ref, k_ref, v_ref, kb_ref, vb_ref):
    h = _rms(mem_ref[...], g_ref[...]).astype(BF16)
    kv = _dot(h, w_ref[...])
    k = kv[:, :MEM_WIDTH]
    v = kv[:, MEM_WIDTH:]
    k_ref[...] = k
    v_ref[...] = v
    kb_ref[...] = k.astype(BF16)
    vb_ref[...] = v.astype(BF16)


def _mem_kv(mem, mem_norm, w_mem_kv):
    b = mem.shape[0]
    out_f32 = jax.ShapeDtypeStruct((b, MEM_TOKENS, MEM_WIDTH), F32)
    out_bf16 = jax.ShapeDtypeStruct((b, MEM_TOKENS, MEM_WIDTH), BF16)
    blk = pl.BlockSpec((None, MEM_TOKENS, MEM_WIDTH), lambda i: (i, 0, 0))
    return pl.pallas_call(
        _mem_kv_kernel,
        grid=(b,),
        in_specs=[pl.BlockSpec((None, MEM_TOKENS, D_MODEL), lambda i: (i, 0, 0)),
                  _const_spec((1, D_MODEL)),
                  _const_spec((D_MODEL, 2 * MEM_WIDTH))],
        out_specs=[blk, blk, blk, blk],
        out_shape=[out_f32, out_f32, out_bf16, out_bf16],
        compiler_params=_params(1),
        name="mem_kv",
    )(mem, mem_norm, w_mem_kv)


def _project(x, cs, sn, an_ref, w1_ref, qan_ref, wq_ref, kvn_ref, wkb_ref):
    h = _rms(x, an_ref[...]).astype(BF16)

    def seg(a, b):
        return _dot(h, w1_ref[:, a:b])

    qa = _rms(seg(C_QA, C_CKV), qan_ref[...]).astype(BF16)
    q_nope = _dot(qa, wq_ref[:, 0:Q_NOPE_W])
    q_rope = _dot(qa, wq_ref[:, Q_NOPE_W:Q_NOPE_W + Q_ROPE_W])
    q_swap = _dot(qa, wq_ref[:, Q_NOPE_W + Q_ROPE_W:Q_NOPE_W + 2 * Q_ROPE_W])
    heads = []
    for hh in range(N_HEADS):
        sl = slice(hh * LANE, (hh + 1) * LANE)
        q_lat = _dot(q_nope[:, sl].astype(BF16), wkb_ref[hh]) * MLA_SCALE
        q_r = (q_rope[:, sl] * cs + q_swap[:, sl] * sn) * MLA_SCALE
        heads.append((q_lat, q_r))
    ckv = _rms(seg(C_CKV, C_KR), kvn_ref[...])
    zk = seg(C_KR, C_U)
    kr = zk[:, :ROPE_PAD] * cs + zk[:, ROPE_PAD:] * sn
    u = seg(C_U, C_QM)
    qm = seg(C_QM, C_GATE)
    g = seg(C_GATE, C_END)
    sg = g / (1.0 + jnp.exp(-g))
    return heads, ckv, kr, u, qm, sg


def _proj_prompt_kernel(x_ref, cs_ref, sn_ref, an_ref, w1_ref, qan_ref, wq_ref, kvn_ref, wkb_ref,
                        wpool_ref, pscale_ref, mk_ref, mv_ref,
                        q_ref, ckv_ref, kr_ref, kc_ref, pool_ref, cross_ref, sg_ref, pst_ref,
                        uext_ref):
    j = pl.program_id(1)
    tm = PROJ_ROWS
    halo = 16
    heads, ckv, kr, u, qm, sg = _project(
        x_ref[...], cs_ref[...], sn_ref[...], an_ref, w1_ref, qan_ref, wq_ref, kvn_ref, wkb_ref)

    for hh, (q_lat, q_r) in enumerate(heads):
        q_ref[hh, :, 0:KV_RANK] = q_lat.astype(BF16)
        q_ref[hh, :, KV_RANK:QK_WIDTH] = q_r.astype(BF16)
    ckv_ref[...] = ckv
    kr_ref[...] = kr[:, :QK_ROPE_DIM]
    kc_ref[:, 0:KV_RANK] = ckv.astype(BF16)
    kc_ref[:, KV_RANK:QK_WIDTH] = kr.astype(BF16)
    sg_ref[...] = sg.astype(BF16)

    @pl.when(j == 0)
    def _():
        uext_ref[0:halo, :] = jnp.zeros((halo, POOL_WIDTH), F32)

    uext_ref[halo:halo + tm, :] = u
    pos = j * tm + lax.broadcasted_iota(jnp.int32, (tm, 1), 0)
    for g, w in enumerate(POOL_WINDOWS):
        sl = slice(g * POOL_GROUP_DIM, (g + 1) * POOL_GROUP_DIM)
        acc = u[:, sl]
        for k in range(1, w):
            acc = acc + uext_ref[halo - k:halo - k + tm, sl]
        cnt = jnp.minimum(pos + 1, w).astype(F32)
        diff = acc / cnt - u[:, sl]
        y = _dot(diff.astype(BF16), wpool_ref[g]) * pscale_ref[:, sl]
        pool_ref[:, sl] = y.astype(BF16)

    @pl.when(j == pl.num_programs(1) - 1)
    def _():
        pst_ref[...] = uext_ref[halo + tm - POOL_STATE_LEN:halo + tm, :]

    uext_ref[0:halo, :] = uext_ref[tm:tm + halo, :]

    for hh in range(MEM_HEADS):
        sl = slice(hh * MEM_HEAD_DIM, (hh + 1) * MEM_HEAD_DIM)
        s = _dot_nt(qm[:, sl].astype(BF16), mk_ref[:, sl]) * MEM_SCALE
        p = _softmax_rows(s).astype(BF16)
        cross_ref[:, sl] = _dot(p, mv_ref[:, sl]).astype(BF16)


def _proj_prompt(x, cs, sn, wts, mkb, mvb):
    b, t, _ = x.shape
    tm = PROJ_ROWS
    nj = t // tm

    def rows(width, dtype):
        return (pl.BlockSpec((None, tm, width), lambda i, j: (i, j, 0)),
                jax.ShapeDtypeStruct((b, t, width), dtype))

    outs = [
        (pl.BlockSpec((None, N_HEADS, tm, QK_WIDTH), lambda i, j: (i, 0, j, 0)),
         jax.ShapeDtypeStruct((b, N_HEADS, t, QK_WIDTH), BF16)),
        rows(KV_RANK, F32),
        rows(QK_ROPE_DIM, F32),
        rows(QK_WIDTH, BF16),
        rows(POOL_WIDTH, BF16),
        rows(MEM_WIDTH, BF16),
        rows(D_MIX, BF16),
        (pl.BlockSpec((None, POOL_STATE_LEN, POOL_WIDTH), lambda i, j: (i, 0, 0)),
         jax.ShapeDtypeStruct((b, POOL_STATE_LEN, POOL_WIDTH), F32)),
    ]
    mem_spec = pl.BlockSpec((None, MEM_TOKENS, MEM_WIDTH), lambda i, j: (i, 0, 0))
    in_specs = [
        pl.BlockSpec((None, tm, D_MODEL), lambda i, j: (i, j, 0)),
        pl.BlockSpec((tm, ROPE_PAD), lambda i, j: (j, 0)),
        pl.BlockSpec((tm, ROPE_PAD), lambda i, j: (j, 0)),
    ] + [_const_spec(w.shape) for w in wts] + [mem_spec, mem_spec]
    return pl.pallas_call(
        _proj_prompt_kernel,
        grid=(b, nj),
        in_specs=in_specs,
        out_specs=[o[0] for o in outs],
        out_shape=[o[1] for o in outs],
        scratch_shapes=[pltpu.VMEM((16 + tm, POOL_WIDTH), F32)],
        compiler_params=_params(2),
        name="proj_prompt",
    )(x, cs, sn, *wts, mkb, mvb)


def _attn_prompt_kernel(q_ref, kc_ref, wvb_ref, o_ref, s_even, s_odd, m_ref, l_ref, acc_ref):
    i = pl.program_id(1)
    rows = N_HEADS * ATT_Q
    sub = ATT_SUB_HEADS * ATT_Q
    n_sub = N_HEADS // ATT_SUB_HEADS
    m_ref[...] = jnp.full((rows, 1), NEG_BIG, F32)
    l_ref[...] = jnp.zeros((rows, 1), F32)
    acc_ref[...] = jnp.zeros((rows, KV_RANK), F32)

    def keys(c):
        return kc_ref[pl.ds(pl.multiple_of(c * ATT_K, ATT_K), ATT_K), :]

    def scores(c, s_ref):
        k = keys(c)
        for sb in range(n_sub):
            q = q_ref[sb * ATT_SUB_HEADS:(sb + 1) * ATT_SUB_HEADS].reshape(sub, QK_WIDTH)
            s_ref[sb * sub:(sb + 1) * sub, :] = _dot_nt(q, k)

    def attend(c, s_ref, masked):
        v = keys(c)[:, 0:KV_RANK]
        if masked:
            key_pos = c * ATT_K + lax.broadcasted_iota(jnp.int32, (ATT_Q, ATT_K), 1)
            q_pos = i * ATT_Q + lax.broadcasted_iota(jnp.int32, (ATT_Q, ATT_K), 0)
            visible = (key_pos <= q_pos)[None]
        for sb in range(n_sub):
            r = slice(sb * sub, (sb + 1) * sub)
            s = s_ref[r, :]
            if masked:
                s = jnp.where(visible, s.reshape(ATT_SUB_HEADS, ATT_Q, ATT_K), NEG_BIG)
                s = s.reshape(sub, ATT_K)
            m_prev = m_ref[r, :]
            m_new = jnp.maximum(m_prev, jnp.max(s, axis=-1, keepdims=True))
            alpha = jnp.exp(m_prev - m_new)
            p = jnp.exp(s - m_new)
            l_ref[r, :] = alpha * l_ref[r, :] + jnp.sum(p, axis=-1, keepdims=True)
            acc_ref[r, :] = alpha * acc_ref[r, :] + _dot(p.astype(BF16), v)
            m_ref[r, :] = m_new

    n_full = (i * ATT_Q) // ATT_K
    scores(0, s_even)

    def body(c, carry):
        @pl.when(lax.rem(c, 2) == 0)
        def _():
            scores(c + 1, s_odd)
            attend(c, s_even, masked=False)

        @pl.when(lax.rem(c, 2) == 1)
        def _():
            scores(c + 1, s_even)
            attend(c, s_odd, masked=False)

        return carry

    lax.fori_loop(0, n_full, body, 0)

    @pl.when(lax.rem(n_full, 2) == 0)
    def _():
        attend(n_full, s_even, masked=True)

    @pl.when(lax.rem(n_full, 2) == 1)
    def _():
        attend(n_full, s_odd, masked=True)

    for hh in range(N_HEADS):
        r = slice(hh * ATT_Q, (hh + 1) * ATT_Q)
        o_h = (acc_ref[r, :] / l_ref[r, :]).astype(BF16)
        o_ref[:, hh * V_HEAD_DIM:(hh + 1) * V_HEAD_DIM] = _dot(o_h, wvb_ref[hh]).astype(BF16)


def _attn_prompt(q, kc, wvb):
    b, _, t, _ = q.shape
    rows = N_HEADS * ATT_Q
    return pl.pallas_call(
        _attn_prompt_kernel,
        grid=(b, t // ATT_Q),
        in_specs=[pl.BlockSpec((None, N_HEADS, ATT_Q, QK_WIDTH), lambda bi, i: (bi, 0, i, 0)),
                  pl.BlockSpec((None, t, QK_WIDTH), lambda bi, i: (bi, 0, 0)),
                  _const_spec(wvb.shape)],
        out_specs=pl.BlockSpec((None, ATT_Q, MLA_WIDTH), lambda bi, i: (bi, i, 0)),
        out_shape=jax.ShapeDtypeStruct((b, t, MLA_WIDTH), BF16),
        scratch_shapes=[pltpu.VMEM((rows, ATT_K), F32), pltpu.VMEM((rows, ATT_K), F32),
                        pltpu.VMEM((rows, 1), F32), pltpu.VMEM((rows, 1), F32),
                        pltpu.VMEM((rows, KV_RANK), F32)],
        compiler_params=_params(2),
        name="attn_prompt",
    )(q, kc, wvb)


def _combine_kernel(x_ref, mla_ref, pool_ref, cross_ref, sg_ref, wout_ref, fn_ref, y_ref):
    def part(ref, a, b):
        mix = ref[...].astype(F32) * sg_ref[:, a:b].astype(F32)
        return _dot(mix.astype(BF16), wout_ref[a:b, :])

    acc = x_ref[...] + part(mla_ref, 0, MLA_WIDTH)
    acc = acc + part(pool_ref, MLA_WIDTH, MLA_WIDTH + POOL_WIDTH)
    acc = acc + part(cross_ref, MLA_WIDTH + POOL_WIDTH, D_MIX)
    y_ref[...] = _rms(acc, fn_ref[...])


def _combine(x, mla, pool, cross, sg, wout, fn):
    n = x.shape[0]
    tm = PROJ_ROWS

    def rows(width):
        return pl.BlockSpec((tm, width), lambda i: (i, 0))

    return pl.pallas_call(
        _combine_kernel,
        grid=(n // tm,),
        in_specs=[rows(D_MODEL), rows(MLA_WIDTH), rows(POOL_WIDTH), rows(MEM_WIDTH), rows(D_MIX),
                  _const_spec(wout.shape), _const_spec(fn.shape)],
        out_specs=rows(D_MODEL),
        out_shape=jax.ShapeDtypeStruct((n, D_MODEL), F32),
        compiler_params=_params(1),
        name="combine",
    )(x, mla, pool, cross, sg, wout, fn)


def _proj_sample_kernel(x_ref, cs_ref, sn_ref, an_ref, w1_ref, qan_ref, wq_ref, kvn_ref, wkb_ref,
                        wpool_ref, pscale_ref, st_ref,
                        q_ref, ckv_ref, kr_ref, kc_ref, pool_ref, qm_ref, sg_ref, pst_ref,
                        ext_ref):
    tm = PROJ_ROWS
    ns = tm // DEC_SEQ
    cs = jnp.broadcast_to(cs_ref[...][None], (ns, DEC_SEQ, ROPE_PAD)).reshape(tm, ROPE_PAD)
    sn = jnp.broadcast_to(sn_ref[...][None], (ns, DEC_SEQ, ROPE_PAD)).reshape(tm, ROPE_PAD)
    heads, ckv, kr, u, qm, sg = _project(
        x_ref[...], cs, sn, an_ref, w1_ref, qan_ref, wq_ref, kvn_ref, wkb_ref)

    for hh, (q_lat, q_r) in enumerate(heads):
        q_ref[:, hh, :, 0:KV_RANK] = q_lat.reshape(ns, DEC_SEQ, KV_RANK)
        q_ref[:, hh, :, KV_RANK:QK_WIDTH] = q_r.reshape(ns, DEC_SEQ, ROPE_PAD)
    ckv_ref[...] = ckv
    kr_ref[...] = kr[:, :QK_ROPE_DIM]
    kc_ref[:, 0:KV_RANK] = ckv
    kc_ref[:, KV_RANK:QK_WIDTH] = kr
    qm_ref[...] = qm
    sg_ref[...] = sg.astype(BF16)

    base = POOL_STATE_LEN + 1
    ext_ref[:, 1:base, :] = st_ref[...]
    ext_ref[:, base:base + DEC_SEQ, :] = u.reshape(ns, DEC_SEQ, POOL_WIDTH)
    for g, w in enumerate(POOL_WINDOWS):
        sl = slice(g * POOL_GROUP_DIM, (g + 1) * POOL_GROUP_DIM)
        acc = ext_ref[:, base:base + DEC_SEQ, sl]
        for k in range(1, w):
            acc = acc + ext_ref[:, base - k:base - k + DEC_SEQ, sl]
        diff = (acc * (1.0 / w)).reshape(tm, POOL_GROUP_DIM) - u[:, sl]
        y = _dot(diff.astype(BF16), wpool_ref[g]) * pscale_ref[:, sl]
        pool_ref[:, sl] = y.astype(BF16)
    pst_ref[...] = ext_ref[:, base + DEC_SEQ - POOL_STATE_LEN:base + DEC_SEQ, :]


def _proj_sample(x, cs, sn, wts, state):
    n = x.shape[0]
    tm = PROJ_ROWS
    ns = tm // DEC_SEQ
    nseq = n // DEC_SEQ

    def rows(width, dtype):
        return (pl.BlockSpec((tm, width), lambda i: (i, 0)), jax.ShapeDtypeStruct((n, width), dtype))

    outs = [
        (pl.BlockSpec((ns, N_HEADS, DEC_SEQ, QK_WIDTH), lambda i: (i, 0, 0, 0)),
         jax.ShapeDtypeStruct((nseq, N_HEADS, DEC_SEQ, QK_WIDTH), F32)),
        rows(KV_RANK, F32),
        rows(QK_ROPE_DIM, F32),
        rows(QK_WIDTH, F32),
        rows(POOL_WIDTH, BF16),
        rows(MEM_WIDTH, F32),
        rows(D_MIX, BF16),
        (pl.BlockSpec((ns, POOL_STATE_LEN, POOL_WIDTH), lambda i: (i, 0, 0)),
         jax.ShapeDtypeStruct((nseq, POOL_STATE_LEN, POOL_WIDTH), F32)),
    ]
    in_specs = [
        pl.BlockSpec((tm, D_MODEL), lambda i: (i, 0)),
        _const_spec(cs.shape),
        _const_spec(sn.shape),
    ] + [_const_spec(w.shape) for w in wts] + [
        pl.BlockSpec((ns, POOL_STATE_LEN, POOL_WIDTH), lambda i: (i, 0, 0))]
    return pl.pallas_call(
        _proj_sample_kernel,
        grid=(n // tm,),
        in_specs=in_specs,
        out_specs=[o[0] for o in outs],
        out_shape=[o[1] for o in outs],
        scratch_shapes=[pltpu.VMEM((ns, POOL_STATE_LEN + 1 + DEC_SEQ, POOL_WIDTH), F32)],
        compiler_params=_params(1),
        name="proj_sample",
    )(x, cs, sn, *wts, state)


def _cross_sample_kernel(qm_ref, k_ref, v_ref, o_ref):
    rows = MEM_HEADS * DEC_SEQ
    cols = MEM_TOKENS * MEM_HEADS
    r_head = lax.broadcasted_iota(jnp.int32, (rows, cols), 0) // DEC_SEQ
    c_head = lax.broadcasted_iota(jnp.int32, (rows, cols), 1) % MEM_HEADS
    same_head = r_head == c_head
    outs = []
    for s_i in range(CROSS_SEQS):
        q = qm_ref[s_i * DEC_SEQ:(s_i + 1) * DEC_SEQ, :]
        q_ht = jnp.concatenate([q[:, hh * MEM_HEAD_DIM:(hh + 1) * MEM_HEAD_DIM]
                                for hh in range(MEM_HEADS)], axis=0).astype(BF16)
        s = jnp.where(same_head, _dot_nt(q_ht, k_ref[s_i].astype(BF16)) * MEM_SCALE, NEG_BIG)
        o_ht = _dot(_softmax_rows(s).astype(BF16), v_ref[s_i].astype(BF16))
        outs.append(jnp.concatenate([o_ht[hh * DEC_SEQ:(hh + 1) * DEC_SEQ]
                                     for hh in range(MEM_HEADS)], axis=-1))
    o_ref[...] = jnp.concatenate(outs, axis=0).astype(BF16)


def _cross_sample(qm, mem_k, mem_v):
    n = qm.shape[0]
    tm = CROSS_SEQS * DEC_SEQ
    kv_spec = pl.BlockSpec((CROSS_SEQS, MEM_TOKENS * MEM_HEADS, MEM_HEAD_DIM), lambda i: (i, 0, 0))
    return pl.pallas_call(
        _cross_sample_kernel,
        grid=(n // tm,),
        in_specs=[pl.BlockSpec((tm, MEM_WIDTH), lambda i: (i, 0)), kv_spec, kv_spec],
        out_specs=pl.BlockSpec((tm, MEM_WIDTH), lambda i: (i, 0)),
        out_shape=jax.ShapeDtypeStruct((n, MEM_WIDTH), BF16),
        compiler_params=_params(1),
        name="cross_sample",
    )(qm, mem_k, mem_v)


def _page_copies(pages, ckv_hbm, krt_hbm, ckv_buf, krt_buf, sem, slot):
    copies = []
    for j, pg in enumerate(pages):
        copies.append(pltpu.make_async_copy(
            ckv_hbm.at[pg], ckv_buf.at[slot, pl.ds(j * PAGE_SIZE, PAGE_SIZE), :], sem.at[slot]))
        copies.append(pltpu.make_async_copy(krt_hbm.at[pg], krt_buf.at[slot, j], sem.at[slot]))
    return copies


def _attn_sample_kernel(pt_ref, q_ref, qn_ref, knew_ref, ckv_hbm, krt_hbm, o_ref,
                        ckv_buf, krt_buf, sem, s_even, s_odd, m_ref, l_ref, acc_ref):
    npg = PAGES_PER_STEP
    nch = N_PAGES // npg
    s_i = pl.program_id(0)
    c = pl.program_id(1)
    step = s_i * nch + c
    last_step = pl.num_programs(0) * nch - 1
    rows = N_HEADS * DEC_SEQ

    def advance(seq, chunk):
        wrap = chunk + 1 == nch
        return jnp.where(wrap, seq + 1, seq), jnp.where(wrap, 0, chunk + 1)

    def fetch(seq, chunk, slot):
        pages = [pt_ref[seq, chunk * npg + j] for j in range(npg)]
        return _page_copies(pages, ckv_hbm, krt_hbm, ckv_buf, krt_buf, sem, slot)

    def wait(slot):
        for cp in _page_copies([0] * npg, ckv_hbm, krt_hbm, ckv_buf, krt_buf, sem, slot):
            cp.wait()

    def scores(q_rows, slot, s_ref):
        kc = ckv_buf[slot].astype(BF16)
        krt = jnp.concatenate([krt_buf[slot, j] for j in range(npg)], axis=-1).astype(BF16)
        s_ref[...] = (_dot_nt(q_rows[:, 0:KV_RANK], kc)
                      + _dot(q_rows[:, KV_RANK:KV_RANK + QK_ROPE_DIM], krt))

    seq1, chunk1 = advance(s_i, c)
    seq2, chunk2 = advance(seq1, chunk1)
    slot0 = lax.rem(step, 3)
    slot1 = lax.rem(step + 1, 3)
    slot2 = lax.rem(step + 2, 3)
    q_f32 = q_ref[...].reshape(rows, QK_WIDTH)
    q = q_f32.astype(BF16)

    @pl.when(step == 0)
    def _():
        for cp in fetch(s_i, c, 0):
            cp.start()
        for cp in fetch(seq1, chunk1, 1):
            cp.start()
        wait(0)
        scores(q, 0, s_even)

    @pl.when(step + 2 <= last_step)
    def _():
        for cp in fetch(seq2, chunk2, slot2):
            cp.start()

    @pl.when(step < last_step)
    def _():
        wait(slot1)

    @pl.when(c == 0)
    def _():
        pad = jnp.zeros((PAGE_SIZE - DEC_SEQ, QK_WIDTH), F32)
        kn = jnp.concatenate([knew_ref[...], pad], axis=0).astype(BF16)
        s = _dot_nt(q, kn)
        key_j = lax.broadcasted_iota(jnp.int32, (rows, PAGE_SIZE), 1)
        q_t = lax.broadcasted_iota(jnp.int32, (rows, PAGE_SIZE), 0) % DEC_SEQ
        s = jnp.where(key_j <= q_t, s, NEG_BIG)
        m = jnp.max(s, axis=-1, keepdims=True)
        p = jnp.exp(s - m)
        m_ref[...] = m
        l_ref[...] = jnp.sum(p, axis=-1, keepdims=True)
        acc_ref[...] = _dot(p.astype(BF16), kn[:, 0:KV_RANK])

    q_next = jnp.where(c == nch - 1, qn_ref[...].reshape(rows, QK_WIDTH), q_f32).astype(BF16)

    def attend_and_score_next(s_cur, s_nxt):
        s = s_cur[...]
        m_prev = m_ref[...]
        m_new = jnp.maximum(m_prev, jnp.max(s, axis=-1, keepdims=True))
        alpha = jnp.exp(m_prev - m_new)
        p = jnp.exp(s - m_new)
        l_ref[...] = alpha * l_ref[...] + jnp.sum(p, axis=-1, keepdims=True)
        acc_ref[...] = alpha * acc_ref[...] + _dot(p.astype(BF16), ckv_buf[slot0].astype(BF16))
        m_ref[...] = m_new
        scores(q_next, slot1, s_nxt)

    @pl.when(lax.rem(step, 2) == 0)
    def _():
        attend_and_score_next(s_even, s_odd)

    @pl.when(lax.rem(step, 2) == 1)
    def _():
        attend_and_score_next(s_odd, s_even)

    @pl.when(c == nch - 1)
    def _():
        o_ref[...] = (acc_ref[...] / l_ref[...]).reshape(N_HEADS, DEC_SEQ, KV_RANK)


def _attn_sample(page_table, q, knew, cache_ckv, cache_krope_t):
    nseq = q.shape[0]
    npg = PAGES_PER_STEP
    rows = N_HEADS * DEC_SEQ
    keys = npg * PAGE_SIZE
    q_block = (None, N_HEADS, DEC_SEQ, QK_WIDTH)
    grid_spec = pltpu.PrefetchScalarGridSpec(
        num_scalar_prefetch=1,
        grid=(nseq, N_PAGES // npg),
        in_specs=[pl.BlockSpec(q_block, lambda s, c, pt: (s, 0, 0, 0)),
                  pl.BlockSpec(q_block, lambda s, c, pt: (jnp.minimum(s + 1, nseq - 1), 0, 0, 0)),
                  pl.BlockSpec((DEC_SEQ, QK_WIDTH), lambda s, c, pt: (s, 0)),
                  pl.BlockSpec(memory_space=pl.ANY),
                  pl.BlockSpec(memory_space=pl.ANY)],
        out_specs=pl.BlockSpec((None, N_HEADS, DEC_SEQ, KV_RANK), lambda s, c, pt: (s, 0, 0, 0)),
        scratch_shapes=[pltpu.VMEM((3, keys, KV_RANK), F32),
                        pltpu.VMEM((3, npg, QK_ROPE_DIM, PAGE_SIZE), F32),
                        pltpu.SemaphoreType.DMA((3,)),
                        pltpu.VMEM((rows, keys), F32), pltpu.VMEM((rows, keys), F32),
                        pltpu.VMEM((rows, 1), F32), pltpu.VMEM((rows, 1), F32),
                        pltpu.VMEM((rows, KV_RANK), F32)],
    )
    return pl.pallas_call(
        _attn_sample_kernel,
        grid_spec=grid_spec,
        out_shape=jax.ShapeDtypeStruct((nseq, N_HEADS, DEC_SEQ, KV_RANK), F32),
        compiler_params=_params(2),
        name="attn_sample",
    )(page_table, q, q, knew, cache_ckv, cache_krope_t)


def _vup_sample_kernel(o_ref, wvb_ref, mla_ref):
    ns = PROJ_ROWS // DEC_SEQ
    for hh in range(N_HEADS):
        o_h = o_ref[:, hh, :, :].reshape(ns * DEC_SEQ, KV_RANK).astype(BF16)
        mla_ref[:, hh * V_HEAD_DIM:(hh + 1) * V_HEAD_DIM] = _dot(o_h, wvb_ref[hh]).astype(BF16)


def _vup_sample(o_lat, wvb):
    nseq = o_lat.shape[0]
    ns = PROJ_ROWS // DEC_SEQ
    return pl.pallas_call(
        _vup_sample_kernel,
        grid=(nseq // ns,),
        in_specs=[pl.BlockSpec((ns, N_HEADS, DEC_SEQ, KV_RANK), lambda i: (i, 0, 0, 0)),
                  _const_spec(wvb.shape)],
        out_specs=pl.BlockSpec((PROJ_ROWS, MLA_WIDTH), lambda i: (i, 0)),
        out_shape=jax.ShapeDtypeStruct((nseq * DEC_SEQ, MLA_WIDTH), BF16),
        compiler_params=_params(1),
        name="vup_sample",
    )(o_lat, wvb)


def _rope_tables(pos):
    inv = ROPE_THETA ** (-jnp.arange(0, QK_ROPE_DIM, 2, dtype=F32) / QK_ROPE_DIM)
    ang = pos.astype(F32)[:, None] * inv[None, :]
    cos, sin = jnp.cos(ang), jnp.sin(ang)
    reps = ROPE_PAD // QK_ROPE_DIM
    cs = jnp.tile(jnp.concatenate([cos, cos], axis=-1), (1, reps))
    sn = jnp.tile(jnp.concatenate([-sin, sin], axis=-1), (1, reps))
    return cs, sn


def _swap_halves(w):
    half = w.shape[-1] // 2
    return jnp.concatenate([w[..., half:], w[..., :half]], axis=-1)


def _pad_lanes(w, width):
    return jnp.pad(w, [(0, 0)] * (w.ndim - 1) + [(0, width - w.shape[-1])])


def _layout_weights(attn_norm, w_in, q_a_norm, w_q_b, kv_a_norm, w_k_b, w_pool, pool_scale):
    s0 = Q_RANK
    s1 = s0 + KV_RANK
    s2 = s1 + QK_ROPE_DIM
    w_kr = w_in[:, s1:s2]
    w1 = jnp.concatenate([
        w_in[:, :s1],
        _pad_lanes(w_kr, ROPE_PAD), _pad_lanes(_swap_halves(w_kr), ROPE_PAD),
        w_in[:, s2:]], axis=1).astype(BF16)
    wq_nope = w_q_b[:, :, :QK_NOPE_DIM].reshape(Q_RANK, Q_NOPE_W)
    wq_r = w_q_b[:, :, QK_NOPE_DIM:]
    wq_rope = _pad_lanes(wq_r, ROPE_PAD).reshape(Q_RANK, Q_ROPE_W)
    wq_swap = _pad_lanes(_swap_halves(wq_r), ROPE_PAD).reshape(Q_RANK, Q_ROPE_W)
    wq = jnp.concatenate([wq_nope, wq_rope, wq_swap], axis=1).astype(BF16)
    wkb = jnp.transpose(w_k_b, (1, 2, 0)).astype(BF16)
    return (attn_norm.reshape(1, D_MODEL), w1, q_a_norm.reshape(1, Q_RANK), wq,
            kv_a_norm.reshape(1, KV_RANK), wkb, w_pool.astype(BF16),
            pool_scale.reshape(1, POOL_WIDTH))


def kernel(x_prompt, x_sample, mem_prompt, cache_ckv, cache_krope, page_table, state_pool,
           cache_mem_k, cache_mem_v, attn_norm, mem_norm, w_in, q_a_norm, w_q_b, kv_a_norm,
           w_k_b, w_v_b, w_pool, pool_scale, w_mem_kv, w_out, final_norm):
    assert DEPTH == 1 and attn_norm.shape[0] == 1
    n_phys = cache_ckv.shape[1]
    wts = _layout_weights(attn_norm[0], w_in[0], q_a_norm[0], w_q_b[0], kv_a_norm[0], w_k_b[0],
                          w_pool[0], pool_scale[0])
    wvb = jnp.transpose(w_v_b[0], (1, 0, 2)).astype(BF16)
    wout = w_out[0].astype(BF16)
    fn = final_norm.reshape(1, D_MODEL)

    mk, mv, mkb, mvb = _mem_kv(mem_prompt, mem_norm[0].reshape(1, D_MODEL), w_mem_kv[0].astype(BF16))
    cs_p, sn_p = _rope_tables(jnp.arange(SEQ))
    q_p, ckv_p, kr_p, kc_p, pool_p, cross_p, sg_p, pst_p = _proj_prompt(
        x_prompt, cs_p, sn_p, wts, mkb, mvb)
    mla_p = _attn_prompt(q_p, kc_p, wvb)
    rows_p = BATCH * SEQ
    y_p = _combine(x_prompt.reshape(rows_p, D_MODEL), mla_p.reshape(rows_p, MLA_WIDTH),
                   pool_p.reshape(rows_p, POOL_WIDTH), cross_p.reshape(rows_p, MEM_WIDTH),
                   sg_p.reshape(rows_p, D_MIX), wout, fn)

    rows_s = DEC_BATCH * DEC_SEQ
    cs_s, sn_s = _rope_tables(PAST_LEN + jnp.arange(DEC_SEQ))
    xs = x_sample.reshape(rows_s, D_MODEL)
    q_s, ckv_s, kr_s, kc_s, pool_s, qm_s, sg_s, pst_s = _proj_sample(
        xs, cs_s, sn_s, wts, state_pool[0])
    cross_s = _cross_sample(
        qm_s,
        cache_mem_k.reshape(DEC_BATCH, MEM_TOKENS * MEM_HEADS, MEM_HEAD_DIM),
        cache_mem_v.reshape(DEC_BATCH, MEM_TOKENS * MEM_HEADS, MEM_HEAD_DIM))
    o_lat_s = _attn_sample(page_table, q_s, kc_s,
                           cache_ckv.reshape(n_phys, PAGE_SIZE, KV_RANK),
                           jnp.swapaxes(cache_krope.reshape(n_phys, PAGE_SIZE, QK_ROPE_DIM), 1, 2))
    mla_s = _vup_sample(o_lat_s, wvb)
    y_s = _combine(xs, mla_s, pool_s, cross_s, sg_s, wout, fn)

    return (y_p.reshape(BATCH, SEQ, D_MODEL),
            y_s.reshape(DEC_BATCH, DEC_SEQ, D_MODEL),
            ckv_p[None],
            kr_p[None],
            pst_p[None],
            mk.reshape(1, BATCH, MEM_TOKENS, MEM_HEADS, MEM_HEAD_DIM),
            mv.reshape(1, BATCH, MEM_TOKENS, MEM_HEADS, MEM_HEAD_DIM),
            ckv_s.reshape(1, DEC_BATCH, DEC_SEQ, KV_RANK),
            kr_s.reshape(1, DEC_BATCH, DEC_SEQ, QK_ROPE_DIM),
            pst_s[None])
```

```python
import functools

import jax
import jax.numpy as jnp
from jax import lax
from jax.experimental import pallas as pl
from jax.experimental.pallas import tpu as pltpu

F32 = jnp.float32
BF16 = jnp.bfloat16

D_MODEL = 2048
BATCH = 2
SEQ = 4096
DEPTH = 1
DEC_BATCH = 128
DEC_SEQ = 8
PAST_LEN = 16384
PAGE_SIZE = 128
N_PAGES = PAST_LEN // PAGE_SIZE

N_HEADS = 8
QK_NOPE_DIM = 128
QK_ROPE_DIM = 64
V_HEAD_DIM = 128
Q_RANK = 512
KV_RANK = 256
MLA_WIDTH = N_HEADS * V_HEAD_DIM
MLA_SCALE = (QK_NOPE_DIM + QK_ROPE_DIM) ** -0.5
ROPE_THETA = 10000.0
POOL_GROUPS = 4
POOL_WINDOWS = (2, 4, 8, 16)
POOL_GROUP_DIM = 128
POOL_WIDTH = POOL_GROUPS * POOL_GROUP_DIM
POOL_STATE_LEN = 15
MEM_TOKENS = 256
MEM_HEADS = 4
MEM_HEAD_DIM = 128
MEM_WIDTH = MEM_HEADS * MEM_HEAD_DIM
MEM_SCALE = MEM_HEAD_DIM ** -0.5
D_MIX = MLA_WIDTH + POOL_WIDTH + MEM_WIDTH
RMS_EPS = 1e-6

LANE = 128
ROPE_PAD = LANE
QK_WIDTH = KV_RANK + ROPE_PAD
C_QA = 0
C_CKV = C_QA + Q_RANK
C_KR = C_CKV + KV_RANK
C_U = C_KR + 2 * ROPE_PAD
C_QM = C_U + POOL_WIDTH
C_GATE = C_QM + MEM_WIDTH
C_END = C_GATE + D_MIX
Q_NOPE_W = N_HEADS * QK_NOPE_DIM
Q_ROPE_W = N_HEADS * ROPE_PAD

VMEM_LIMIT = 56 * 1024 * 1024
NEG_BIG = -1e30

PROJ_ROWS = 256
GATE_PARTS = 4
ATT_Q = 256
ATT_K = 512
ATT_SUB_HEADS = 1
PAGES_PER_STEP = 32
CROSS_SEQS = 8


def _const_spec(shape):
    n = len(shape)
    return pl.BlockSpec(shape, lambda *_: (0,) * n, pipeline_mode=pl.Buffered(1))


def _params(n_axes):
    return pltpu.CompilerParams(dimension_semantics=("arbitrary",) * n_axes,
                                vmem_limit_bytes=VMEM_LIMIT)


def _rms(x, g):
    return x * lax.rsqrt(jnp.mean(x * x, axis=-1, keepdims=True) + RMS_EPS) * g


def _dot(a, b):
    return jnp.dot(a, b, preferred_element_type=F32)


def _dot_nt(a, b):
    return lax.dot_general(a, b, (((1,), (1,)), ((), ())), preferred_element_type=F32)


def _softmax_rows(s):
    m = jnp.max(s, axis=-1, keepdims=True)
    e = jnp.exp(s - m)
    return e / jnp.sum(e, axis=-1, keepdims=True)


def _mem_kv_kernel(mem_ref, g_ref, w_ref, k_ref, v_ref, kb_ref, vb_ref):
    h = _rms(mem_ref[...], g_ref[...]).astype(BF16)
    kv = _dot(h, w_ref[...])
    k = kv[:, :MEM_WIDTH]
    v = kv[:, MEM_WIDTH:]
    k_ref[...] = k
    v_ref[...] = v
    kb_ref[...] = k.astype(BF16)
    vb_ref[...] = v.astype(BF16)


def _mem_kv(mem, mem_norm, w_mem_kv):
    b = mem.shape[0]
    out_f32 = jax.ShapeDtypeStruct((b, MEM_TOKENS, MEM_WIDTH), F32)
    out_bf16 = jax.ShapeDtypeStruct((b, MEM_TOKENS, MEM_WIDTH), BF16)
    blk = pl.BlockSpec((None, MEM_TOKENS, MEM_WIDTH), lambda i: (i, 0, 0))
    return pl.pallas_call(
        _mem_kv_kernel,
        grid=(b,),
        in_specs=[pl.BlockSpec((None, MEM_TOKENS, D_MODEL), lambda i: (i, 0, 0)),
                  _const_spec((1, D_MODEL)),
                  _const_spec((D_MODEL, 2 * MEM_WIDTH))],
        out_specs=[blk, blk, blk, blk],
        out_shape=[out_f32, out_f32, out_bf16, out_bf16],
        compiler_params=_params(1),
        name="mem_kv",
    )(mem, mem_norm, w_mem_kv)


def _normed_input(x, an_ref):
    return _rms(x, an_ref[...]).astype(BF16)


def _segment(h, w1_ref, a, b):
    return _dot(h, w1_ref[:, a:b])


def _query_low_rank(h, w1_ref, qan_ref):
    return _rms(_segment(h, w1_ref, C_QA, C_CKV), qan_ref[...]).astype(BF16)


def _query_up(qa, wq_ref):
    q_nope = _dot(qa, wq_ref[:, 0:Q_NOPE_W])
    q_rope = _dot(qa, wq_ref[:, Q_NOPE_W:Q_NOPE_W + Q_ROPE_W])
    q_swap = _dot(qa, wq_ref[:, Q_NOPE_W + Q_ROPE_W:Q_NOPE_W + 2 * Q_ROPE_W])
    return q_nope, q_rope, q_swap


def _query_heads(q_up, cs, sn, wkb_ref):
    q_nope, q_rope, q_swap = q_up
    heads = []
    for pair in range(N_HEADS // 2):
        both = slice(2 * pair * QK_NOPE_DIM, 2 * (pair + 1) * QK_NOPE_DIM)
        q_lat2 = _dot(q_nope[:, both].astype(BF16), wkb_ref[pair]) * MLA_SCALE
        for k in range(2):
            sl = slice((2 * pair + k) * LANE, (2 * pair + k + 1) * LANE)
            q_r = (q_rope[:, sl] * cs + q_swap[:, sl] * sn) * MLA_SCALE
            heads.append((q_lat2[:, k * KV_RANK:(k + 1) * KV_RANK], q_r))
    return heads


def _latent_key(h, cs, sn, w1_ref, kvn_ref):
    z = _segment(h, w1_ref, C_CKV, C_U)
    ckv = _rms(z[:, 0:KV_RANK], kvn_ref[...])
    zk = z[:, KV_RANK:]
    kr = zk[:, :ROPE_PAD] * cs + zk[:, ROPE_PAD:] * sn
    return ckv, kr


def _silu_gate_part(h, w1_ref, sg_ref, part):
    width = D_MIX // GATE_PARTS
    g = _segment(h, w1_ref, C_GATE + part * width, C_GATE + (part + 1) * width)
    sg_ref[:, part * width:(part + 1) * width] = (g / (1.0 + jnp.exp(-g))).astype(BF16)


def _proj_prompt_kernel(x_ref, cs_ref, sn_ref, an_ref, w1_ref, qan_ref, wq_ref, kvn_ref, wkb_ref,
                        wpool_ref, pscale_ref, mk_ref, mv_ref,
                        q_ref, ckv_ref, kr_ref, kc_ref, pool_ref, cross_ref, sg_ref, pst_ref,
                        uext_ref):
    j = pl.program_id(1)
    tm = PROJ_ROWS
    halo = 16

    @pl.when(j == 0)
    def _():
        uext_ref[0:halo, :] = jnp.zeros((halo, POOL_WIDTH), F32)

    cs = cs_ref[...]
    sn = sn_ref[...]
    h = _normed_input(x_ref[...], an_ref)
    u_qm = _segment(h, w1_ref, C_U, C_GATE)
    u = u_qm[:, 0:POOL_WIDTH]
    qm = u_qm[:, POOL_WIDTH:]
    qa = _query_low_rank(h, w1_ref, qan_ref)
    _silu_gate_part(h, w1_ref, sg_ref, 0)

    head_lanes = [slice(hh * MEM_HEAD_DIM, (hh + 1) * MEM_HEAD_DIM) for hh in range(MEM_HEADS)]
    scores = [_dot_nt(qm[:, sl].astype(BF16), mk_ref[:, sl]) for sl in head_lanes]
    probs = [_softmax_rows(s * MEM_SCALE).astype(BF16) for s in scores]
    for sl, p in zip(head_lanes, probs):
        cross_ref[:, sl] = _dot(p, mv_ref[:, sl]).astype(BF16)
    q_up = _query_up(qa, wq_ref)
    _silu_gate_part(h, w1_ref, sg_ref, 1)

    uext_ref[halo:halo + tm, :] = u
    pos = j * tm + lax.broadcasted_iota(jnp.int32, (tm, 1), 0)
    group_lanes = [slice(g * POOL_GROUP_DIM, (g + 1) * POOL_GROUP_DIM) for g in range(POOL_GROUPS)]
    diffs = []
    for sl, w in zip(group_lanes, POOL_WINDOWS):
        acc = u[:, sl]
        for k in range(1, w):
            acc = acc + uext_ref[halo - k:halo - k + tm, sl]
        cnt = jnp.minimum(pos + 1, w).astype(F32)
        diffs.append((acc / cnt - u[:, sl]).astype(BF16))
    for g, (sl, diff) in enumerate(zip(group_lanes, diffs)):
        pool_ref[:, sl] = (_dot(diff, wpool_ref[g]) * pscale_ref[:, sl]).astype(BF16)
    uext_ref[0:halo, :] = uext_ref[tm:tm + halo, :]
    _silu_gate_part(h, w1_ref, sg_ref, 2)

    for hh, (q_lat, q_r) in enumerate(_query_heads(q_up, cs, sn, wkb_ref)):
        q_ref[hh, :, 0:KV_RANK] = q_lat.astype(BF16)
        q_ref[hh, :, KV_RANK:QK_WIDTH] = q_r.astype(BF16)
    _silu_gate_part(h, w1_ref, sg_ref, 3)
    ckv, kr = _latent_key(h, cs, sn, w1_ref, kvn_ref)
    ckv_ref[...] = ckv
    kr_ref[...] = kr[:, :QK_ROPE_DIM]
    kc_ref[:, 0:KV_RANK] = ckv.astype(BF16)
    kc_ref[:, KV_RANK:QK_WIDTH] = kr.astype(BF16)

    @pl.when(j == pl.num_programs(1) - 1)
    def _():
        pst_ref[...] = uext_ref[halo + tm - POOL_STATE_LEN:halo + tm, :]


def _proj_prompt(x, cs, sn, wts, mkb, mvb):
    b, t, _ = x.shape
    tm = PROJ_ROWS
    nj = t // tm

    def rows(width, dtype):
        return (pl.BlockSpec((None, tm, width), lambda i, j: (i, j, 0)),
                jax.ShapeDtypeStruct((b, t, width), dtype))

    outs = [
        (pl.BlockSpec((None, N_HEADS, tm, QK_WIDTH), lambda i, j: (i, 0, j, 0)),
         jax.ShapeDtypeStruct((b, N_HEADS, t, QK_WIDTH), BF16)),
        rows(KV_RANK, F32),
        rows(QK_ROPE_DIM, F32),
        rows(QK_WIDTH, BF16),
        rows(POOL_WIDTH, BF16),
        rows(MEM_WIDTH, BF16),
        rows(D_MIX, BF16),
        (pl.BlockSpec((None, POOL_STATE_LEN, POOL_WIDTH), lambda i, j: (i, 0, 0)),
         jax.ShapeDtypeStruct((b, POOL_STATE_LEN, POOL_WIDTH), F32)),
    ]
    mem_spec = pl.BlockSpec((None, MEM_TOKENS, MEM_WIDTH), lambda i, j: (i, 0, 0))
    in_specs = [
        pl.BlockSpec((None, tm, D_MODEL), lambda i, j: (i, j, 0)),
        pl.BlockSpec((tm, ROPE_PAD), lambda i, j: (j, 0)),
        pl.BlockSpec((tm, ROPE_PAD), lambda i, j: (j, 0)),
    ] + [_const_spec(w.shape) for w in wts] + [mem_spec, mem_spec]
    return pl.pallas_call(
        _proj_prompt_kernel,
        grid=(b, nj),
        in_specs=in_specs,
        out_specs=[o[0] for o in outs],
        out_shape=[o[1] for o in outs],
        scratch_shapes=[pltpu.VMEM((16 + tm, POOL_WIDTH), F32)],
        compiler_params=_params(2),
        name="proj_prompt",
    )(x, cs, sn, *wts, mkb, mvb)


def _attn_prompt_kernel(q_ref, kc_ref, wvb_ref, o_ref, s_even, s_odd, m_ref, l_ref, acc_ref):
    i = pl.program_id(1)
    rows = N_HEADS * ATT_Q
    sub = ATT_SUB_HEADS * ATT_Q
    n_sub = N_HEADS // ATT_SUB_HEADS
    m_ref[...] = jnp.full((rows, 1), NEG_BIG, F32)
    l_ref[...] = jnp.zeros((rows, 1), F32)
    acc_ref[...] = jnp.zeros((rows, KV_RANK), F32)

    def keys(c):
        return kc_ref[pl.ds(pl.multiple_of(c * ATT_K, ATT_K), ATT_K), :]

    def scores(c, s_ref):
        k = keys(c)
        for sb in range(n_sub):
            q = q_ref[sb * ATT_SUB_HEADS:(sb + 1) * ATT_SUB_HEADS].reshape(sub, QK_WIDTH)
            s_ref[sb * sub:(sb + 1) * sub, :] = _dot_nt(q, k)

    def attend(c, s_ref, masked):
        v = keys(c)[:, 0:KV_RANK]
        if masked:
            key_pos = c * ATT_K + lax.broadcasted_iota(jnp.int32, (ATT_Q, ATT_K), 1)
            q_pos = i * ATT_Q + lax.broadcasted_iota(jnp.int32, (ATT_Q, ATT_K), 0)
            visible = (key_pos <= q_pos)[None]
        for sb in range(n_sub):
            r = slice(sb * sub, (sb + 1) * sub)
            s = s_ref[r, :]
            if masked:
                s = jnp.where(visible, s.reshape(ATT_SUB_HEADS, ATT_Q, ATT_K), NEG_BIG)
                s = s.reshape(sub, ATT_K)
            m_prev = m_ref[r, :]
            m_new = jnp.maximum(m_prev, jnp.max(s, axis=-1, keepdims=True))
            alpha = jnp.exp(m_prev - m_new)
            p = jnp.exp(s - m_new)
            l_ref[r, :] = alpha * l_ref[r, :] + jnp.sum(p, axis=-1, keepdims=True)
            acc_ref[r, :] = alpha * acc_ref[r, :] + _dot(p.astype(BF16), v)
            m_ref[r, :] = m_new

    n_full = (i * ATT_Q) // ATT_K
    scores(0, s_even)

    def body(c, carry):
        @pl.when(lax.rem(c, 2) == 0)
        def _():
            scores(c + 1, s_odd)
            attend(c, s_even, masked=False)

        @pl.when(lax.rem(c, 2) == 1)
        def _():
            scores(c + 1, s_even)
            attend(c, s_odd, masked=False)

        return carry

    lax.fori_loop(0, n_full, body, 0)

    @pl.when(lax.rem(n_full, 2) == 0)
    def _():
        attend(n_full, s_even, masked=True)

    @pl.when(lax.rem(n_full, 2) == 1)
    def _():
        attend(n_full, s_odd, masked=True)

    for hh in range(N_HEADS):
        r = slice(hh * ATT_Q, (hh + 1) * ATT_Q)
        o_h = (acc_ref[r, :] / l_ref[r, :]).astype(BF16)
        o_ref[:, hh * V_HEAD_DIM:(hh + 1) * V_HEAD_DIM] = _dot(o_h, wvb_ref[hh]).astype(BF16)


def _attn_prompt(q, kc, wvb):
    b, _, t, _ = q.shape
    rows = N_HEADS * ATT_Q
    return pl.pallas_call(
        _attn_prompt_kernel,
        grid=(b, t // ATT_Q),
        in_specs=[pl.BlockSpec((None, N_HEADS, ATT_Q, QK_WIDTH), lambda bi, i: (bi, 0, i, 0)),
                  pl.BlockSpec((None, t, QK_WIDTH), lambda bi, i: (bi, 0, 0)),
                  _const_spec(wvb.shape)],
        out_specs=pl.BlockSpec((None, ATT_Q, MLA_WIDTH), lambda bi, i: (bi, i, 0)),
        out_shape=jax.ShapeDtypeStruct((b, t, MLA_WIDTH), BF16),
        scratch_shapes=[pltpu.VMEM((rows, ATT_K), F32), pltpu.VMEM((rows, ATT_K), F32),
                        pltpu.VMEM((rows, 1), F32), pltpu.VMEM((rows, 1), F32),
                        pltpu.VMEM((rows, KV_RANK), F32)],
        compiler_params=_params(2),
        name="attn_prompt",
    )(q, kc, wvb)


def _combine_kernel(x_ref, mla_ref, pool_ref, cross_ref, sg_ref, wout_ref, fn_ref, y_ref):
    def part(ref, a, b):
        mix = ref[...].astype(F32) * sg_ref[:, a:b].astype(F32)
        return _dot(mix.astype(BF16), wout_ref[a:b, :])

    acc = x_ref[...] + part(mla_ref, 0, MLA_WIDTH)
    acc = acc + part(pool_ref, MLA_WIDTH, MLA_WIDTH + POOL_WIDTH)
    acc = acc + part(cross_ref, MLA_WIDTH + POOL_WIDTH, D_MIX)
    y_ref[...] = _rms(acc, fn_ref[...])


def _combine(x, mla, pool, cross, sg, wout, fn):
    n = x.shape[0]
    tm = PROJ_ROWS

    def rows(width):
        return pl.BlockSpec((tm, width), lambda i: (i, 0))

    return pl.pallas_call(
        _combine_kernel,
        grid=(n // tm,),
        in_specs=[rows(D_MODEL), rows(MLA_WIDTH), rows(POOL_WIDTH), rows(MEM_WIDTH), rows(D_MIX),
                  _const_spec(wout.shape), _const_spec(fn.shape)],
        out_specs=rows(D_MODEL),
        out_shape=jax.ShapeDtypeStruct((n, D_MODEL), F32),
        compiler_params=_params(1),
        name="combine",
    )(x, mla, pool, cross, sg, wout, fn)


def _proj_sample_kernel(x_ref, cs_ref, sn_ref, an_ref, w1_ref, qan_ref, wq_ref, kvn_ref, wkb_ref,
                        wpool_ref, pscale_ref, st_ref,
                        q_ref, ckv_ref, kr_ref, kc_ref, pool_ref, qm_ref, sg_ref, pst_ref,
                        ext_ref):
    tm = PROJ_ROWS
    ns = tm // DEC_SEQ
    cs = jnp.broadcast_to(cs_ref[...][None], (ns, DEC_SEQ, ROPE_PAD)).reshape(tm, ROPE_PAD)
    sn = jnp.broadcast_to(sn_ref[...][None], (ns, DEC_SEQ, ROPE_PAD)).reshape(tm, ROPE_PAD)
    h = _normed_input(x_ref[...], an_ref)
    u_qm = _segment(h, w1_ref, C_U, C_GATE)
    u = u_qm[:, 0:POOL_WIDTH]
    qm_ref[...] = u_qm[:, POOL_WIDTH:]
    q_up = _query_up(_query_low_rank(h, w1_ref, qan_ref), wq_ref)
    for hh, (q_lat, q_r) in enumerate(_query_heads(q_up, cs, sn, wkb_ref)):
        q_ref[:, hh, :, 0:KV_RANK] = q_lat.reshape(ns, DEC_SEQ, KV_RANK)
        q_ref[:, hh, :, KV_RANK:QK_WIDTH] = q_r.reshape(ns, DEC_SEQ, ROPE_PAD)
    ckv, kr = _latent_key(h, cs, sn, w1_ref, kvn_ref)
    ckv_ref[...] = ckv
    kr_ref[...] = kr[:, :QK_ROPE_DIM]
    kc_ref[:, 0:KV_RANK] = ckv
    kc_ref[:, KV_RANK:QK_WIDTH] = kr
    for part in range(GATE_PARTS):
        _silu_gate_part(h, w1_ref, sg_ref, part)

    base = POOL_STATE_LEN + 1
    ext_ref[:, 1:base, :] = st_ref[...]
    ext_ref[:, base:base + DEC_SEQ, :] = u.reshape(ns, DEC_SEQ, POOL_WIDTH)
    for g, w in enumerate(POOL_WINDOWS):
        sl = slice(g * POOL_GROUP_DIM, (g + 1) * POOL_GROUP_DIM)
        acc = ext_ref[:, base:base + DEC_SEQ, sl]
        for k in range(1, w):
            acc = acc + ext_ref[:, base - k:base - k + DEC_SEQ, sl]
        diff = (acc * (1.0 / w)).reshape(tm, POOL_GROUP_DIM) - u[:, sl]
        y = _dot(diff.astype(BF16), wpool_ref[g]) * pscale_ref[:, sl]
        pool_ref[:, sl] = y.astype(BF16)
    pst_ref[...] = ext_ref[:, base + DEC_SEQ - POOL_STATE_LEN:base + DEC_SEQ, :]


def _proj_sample(x, cs, sn, wts, state):
    n = x.shape[0]
    tm = PROJ_ROWS
    ns = tm // DEC_SEQ
    nseq = n // DEC_SEQ

    def rows(width, dtype):
        return (pl.BlockSpec((tm, width), lambda i: (i, 0)), jax.ShapeDtypeStruct((n, width), dtype))

    outs = [
        (pl.BlockSpec((ns, N_HEADS, DEC_SEQ, QK_WIDTH), lambda i: (i, 0, 0, 0)),
         jax.ShapeDtypeStruct((nseq, N_HEADS, DEC_SEQ, QK_WIDTH), F32)),
        rows(KV_RANK, F32),
        rows(QK_ROPE_DIM, F32),
        rows(QK_WIDTH, F32),
        rows(POOL_WIDTH, BF16),
        rows(MEM_WIDTH, F32),
        rows(D_MIX, BF16),
        (pl.BlockSpec((ns, POOL_STATE_LEN, POOL_WIDTH), lambda i: (i, 0, 0)),
         jax.ShapeDtypeStruct((nseq, POOL_STATE_LEN, POOL_WIDTH), F32)),
    ]
    in_specs = [
        pl.BlockSpec((tm, D_MODEL), lambda i: (i, 0)),
        _const_spec(cs.shape),
        _const_spec(sn.shape),
    ] + [_const_spec(w.shape) for w in wts] + [
        pl.BlockSpec((ns, POOL_STATE_LEN, POOL_WIDTH), lambda i: (i, 0, 0))]
    return pl.pallas_call(
        _proj_sample_kernel,
        grid=(n // tm,),
        in_specs=in_specs,
        out_specs=[o[0] for o in outs],
        out_shape=[o[1] for o in outs],
        scratch_shapes=[pltpu.VMEM((ns, POOL_STATE_LEN + 1 + DEC_SEQ, POOL_WIDTH), F32)],
        compiler_params=_params(1),
        name="proj_sample",
    )(x, cs, sn, *wts, state)


def _cross_sample_kernel(qm_ref, k_ref, v_ref, o_ref):
    rows = MEM_HEADS * DEC_SEQ
    cols = MEM_TOKENS * MEM_HEADS
    r_head = lax.broadcasted_iota(jnp.int32, (rows, cols), 0) // DEC_SEQ
    c_head = lax.broadcasted_iota(jnp.int32, (rows, cols), 1) % MEM_HEADS
    same_head = r_head == c_head
    scores = []
    for s_i in range(CROSS_SEQS):
        q = qm_ref[s_i * DEC_SEQ:(s_i + 1) * DEC_SEQ, :]
        q_ht = jnp.concatenate([q[:, hh * MEM_HEAD_DIM:(hh + 1) * MEM_HEAD_DIM]
                                for hh in range(MEM_HEADS)], axis=0).astype(BF16)
        scores.append(_dot_nt(q_ht, k_ref[s_i].astype(BF16)))
    probs = [_softmax_rows(jnp.where(same_head, s * MEM_SCALE, NEG_BIG)).astype(BF16) for s in scores]
    outs = []
    for s_i in range(CROSS_SEQS):
        o_ht = _dot(probs[s_i], v_ref[s_i].astype(BF16))
        outs.append(jnp.concatenate([o_ht[hh * DEC_SEQ:(hh + 1) * DEC_SEQ]
                                     for hh in range(MEM_HEADS)], axis=-1))
    o_ref[...] = jnp.concatenate(outs, axis=0).astype(BF16)


def _cross_sample(qm, mem_k, mem_v):
    n = qm.shape[0]
    tm = CROSS_SEQS * DEC_SEQ
    kv_spec = pl.BlockSpec((CROSS_SEQS, MEM_TOKENS * MEM_HEADS, MEM_HEAD_DIM), lambda i: (i, 0, 0))
    return pl.pallas_call(
        _cross_sample_kernel,
        grid=(n // tm,),
        in_specs=[pl.BlockSpec((tm, MEM_WIDTH), lambda i: (i, 0)), kv_spec, kv_spec],
        out_specs=pl.BlockSpec((tm, MEM_WIDTH), lambda i: (i, 0)),
        out_shape=jax.ShapeDtypeStruct((n, MEM_WIDTH), BF16),
        compiler_params=_params(1),
        name="cross_sample",
    )(qm, mem_k, mem_v)


def _page_copies(pages, ckv_hbm, krt_hbm, ckv_buf, krt_buf, sem, slot):
    copies = []
    for j, pg in enumerate(pages):
        copies.append(pltpu.make_async_copy(
            ckv_hbm.at[pg], ckv_buf.at[slot, pl.ds(j * PAGE_SIZE, PAGE_SIZE), :], sem.at[slot]))
        copies.append(pltpu.make_async_copy(krt_hbm.at[pg], krt_buf.at[slot, j], sem.at[slot]))
    return copies


def _attn_sample_kernel(pt_ref, q_ref, qn_ref, knew_ref, ckv_hbm, krt_hbm, o_ref,
                        ckv_buf, krt_buf, sem, s_buf, kcb_buf, m_ref, l_ref, acc_ref):
    npg = PAGES_PER_STEP
    nch = N_PAGES // npg
    s_i = pl.program_id(0)
    c = pl.program_id(1)
    step = s_i * nch + c
    last_step = pl.num_programs(0) * nch - 1
    parity = lax.rem(step, 2)
    rows = N_HEADS * DEC_SEQ

    def advance(seq, chunk):
        wrap = chunk + 1 == nch
        return jnp.where(wrap, seq + 1, seq), jnp.where(wrap, 0, chunk + 1)

    def start_fetch(seq, chunk, slot):
        pages = [pt_ref[seq, chunk * npg + j] for j in range(npg)]
        for n, cp in enumerate(_page_copies(pages, ckv_hbm, krt_hbm, ckv_buf, krt_buf, sem, slot)):
            cp.start(priority=(n // 2) % 2)

    def wait(slot):
        for cp in _page_copies([0] * npg, ckv_hbm, krt_hbm, ckv_buf, krt_buf, sem, slot):
            cp.wait()

    def scores(q_rows, p):
        kc = ckv_buf[p].astype(BF16)
        krt = jnp.concatenate([krt_buf[p, j] for j in range(npg)], axis=-1).astype(BF16)
        kcb_buf[p] = kc
        s_buf[p] = (_dot_nt(q_rows[:, 0:KV_RANK], kc)
                    + _dot(q_rows[:, KV_RANK:KV_RANK + QK_ROPE_DIM], krt))

    seq1, chunk1 = advance(s_i, c)
    seq2, chunk2 = advance(seq1, chunk1)
    q_f32 = q_ref[...].reshape(rows, QK_WIDTH)
    q = q_f32.astype(BF16)

    @pl.when(step == 0)
    def _():
        start_fetch(s_i, c, 0)
        start_fetch(seq1, chunk1, 1)
        wait(0)
        scores(q, 0)

    @pl.when(step + 2 <= last_step)
    def _():
        start_fetch(seq2, chunk2, parity)

    @pl.when(step < last_step)
    def _():
        wait(1 - parity)

    @pl.when(c == 0)
    def _():
        pad = jnp.zeros((PAGE_SIZE - DEC_SEQ, QK_WIDTH), F32)
        kn = jnp.concatenate([knew_ref[...], pad], axis=0).astype(BF16)
        s = _dot_nt(q, kn)
        key_j = lax.broadcasted_iota(jnp.int32, (rows, PAGE_SIZE), 1)
        q_t = lax.broadcasted_iota(jnp.int32, (rows, PAGE_SIZE), 0) % DEC_SEQ
        s = jnp.where(key_j <= q_t, s, NEG_BIG)
        m = jnp.max(s, axis=-1, keepdims=True)
        p = jnp.exp(s - m)
        m_ref[...] = m
        l_ref[...] = jnp.sum(p, axis=-1, keepdims=True)
        acc_ref[...] = _dot(p.astype(BF16), kn[:, 0:KV_RANK])

    q_next = jnp.where(c == nch - 1, qn_ref[...].reshape(rows, QK_WIDTH), q_f32).astype(BF16)

    def attend_and_score_next(p):
        scores(q_next, 1 - p)
        s = s_buf[p]
        m_prev = m_ref[...]
        m_new = jnp.maximum(m_prev, jnp.max(s, axis=-1, keepdims=True))
        alpha = jnp.exp(m_prev - m_new)
        prob = jnp.exp(s - m_new)
        l_ref[...] = alpha * l_ref[...] + jnp.sum(prob, axis=-1, keepdims=True)
        acc_ref[...] = alpha * acc_ref[...] + _dot(prob.astype(BF16), kcb_buf[p])
        m_ref[...] = m_new

    @pl.when(parity == 0)
    def _():
        attend_and_score_next(0)

    @pl.when(parity == 1)
    def _():
        attend_and_score_next(1)

    @pl.when(c == nch - 1)
    def _():
        o_ref[...] = (acc_ref[...] / l_ref[...]).reshape(N_HEADS, DEC_SEQ, KV_RANK)


def _attn_sample(page_table, q, knew, cache_ckv, cache_krope_t):
    nseq = q.shape[0]
    npg = PAGES_PER_STEP
    rows = N_HEADS * DEC_SEQ
    keys = npg * PAGE_SIZE
    q_block = (None, N_HEADS, DEC_SEQ, QK_WIDTH)
    grid_spec = pltpu.PrefetchScalarGridSpec(
        num_scalar_prefetch=1,
        grid=(nseq, N_PAGES // npg),
        in_specs=[pl.BlockSpec(q_block, lambda s, c, pt: (s, 0, 0, 0)),
                  pl.BlockSpec(q_block, lambda s, c, pt: (jnp.minimum(s + 1, nseq - 1), 0, 0, 0)),
                  pl.BlockSpec((DEC_SEQ, QK_WIDTH), lambda s, c, pt: (s, 0)),
                  pl.BlockSpec(memory_space=pl.ANY),
                  pl.BlockSpec(memory_space=pl.ANY)],
        out_specs=pl.BlockSpec((None, N_HEADS, DEC_SEQ, KV_RANK), lambda s, c, pt: (s, 0, 0, 0)),
        scratch_shapes=[pltpu.VMEM((2, keys, KV_RANK), F32),
                        pltpu.VMEM((2, npg, QK_ROPE_DIM, PAGE_SIZE), F32),
                        pltpu.SemaphoreType.DMA((2,)),
                        pltpu.VMEM((2, rows, keys), F32),
                        pltpu.VMEM((2, keys, KV_RANK), BF16),
                        pltpu.VMEM((rows, 1), F32), pltpu.VMEM((rows, 1), F32),
                        pltpu.VMEM((rows, KV_RANK), F32)],
    )
    return pl.pallas_call(
        _attn_sample_kernel,
        grid_spec=grid_spec,
        out_shape=jax.ShapeDtypeStruct((nseq, N_HEADS, DEC_SEQ, KV_RANK), F32),
        compiler_params=_params(2),
        name="attn_sample",
    )(page_table, q, q, knew, cache_ckv, cache_krope_t)


def _vup_sample_kernel(o_ref, wvb_ref, mla_ref):
    ns = PROJ_ROWS // DEC_SEQ
    for hh in range(N_HEADS):
        o_h = o_ref[:, hh, :, :].reshape(ns * DEC_SEQ, KV_RANK).astype(BF16)
        mla_ref[:, hh * V_HEAD_DIM:(hh + 1) * V_HEAD_DIM] = _dot(o_h, wvb_ref[hh]).astype(BF16)


def _vup_sample(o_lat, wvb):
    nseq = o_lat.shape[0]
    ns = PROJ_ROWS // DEC_SEQ
    return pl.pallas_call(
        _vup_sample_kernel,
        grid=(nseq // ns,),
        in_specs=[pl.BlockSpec((ns, N_HEADS, DEC_SEQ, KV_RANK), lambda i: (i, 0, 0, 0)),
                  _const_spec(wvb.shape)],
        out_specs=pl.BlockSpec((PROJ_ROWS, MLA_WIDTH), lambda i: (i, 0)),
        out_shape=jax.ShapeDtypeStruct((nseq * DEC_SEQ, MLA_WIDTH), BF16),
        compiler_params=_params(1),
        name="vup_sample",
    )(o_lat, wvb)


def _rope_tables(pos):
    inv = ROPE_THETA ** (-jnp.arange(0, QK_ROPE_DIM, 2, dtype=F32) / QK_ROPE_DIM)
    ang = pos.astype(F32)[:, None] * inv[None, :]
    cos, sin = jnp.cos(ang), jnp.sin(ang)
    reps = ROPE_PAD // QK_ROPE_DIM
    cs = jnp.tile(jnp.concatenate([cos, cos], axis=-1), (1, reps))
    sn = jnp.tile(jnp.concatenate([-sin, sin], axis=-1), (1, reps))
    return cs, sn


def _swap_halves(w):
    half = w.shape[-1] // 2
    return jnp.concatenate([w[..., half:], w[..., :half]], axis=-1)


def _pad_lanes(w, width):
    return jnp.pad(w, [(0, 0)] * (w.ndim - 1) + [(0, width - w.shape[-1])])


def _layout_weights(attn_norm, w_in, q_a_norm, w_q_b, kv_a_norm, w_k_b, w_pool, pool_scale):
    s0 = Q_RANK
    s1 = s0 + KV_RANK
    s2 = s1 + QK_ROPE_DIM
    w_kr = w_in[:, s1:s2]
    w1 = jnp.concatenate([
        w_in[:, :s1],
        _pad_lanes(w_kr, ROPE_PAD), _pad_lanes(_swap_halves(w_kr), ROPE_PAD),
        w_in[:, s2:]], axis=1).astype(BF16)
    wq_nope = w_q_b[:, :, :QK_NOPE_DIM].reshape(Q_RANK, Q_NOPE_W)
    wq_r = w_q_b[:, :, QK_NOPE_DIM:]
    wq_rope = _pad_lanes(wq_r, ROPE_PAD).reshape(Q_RANK, Q_ROPE_W)
    wq_swap = _pad_lanes(_swap_halves(wq_r), ROPE_PAD).reshape(Q_RANK, Q_ROPE_W)
    wq = jnp.concatenate([wq_nope, wq_rope, wq_swap], axis=1).astype(BF16)
    wkb = jnp.transpose(w_k_b, (1, 2, 0)).astype(BF16)
    zero = jnp.zeros_like(wkb[0::2])
    wkb = jnp.concatenate([jnp.concatenate([wkb[0::2], zero], axis=2),
                           jnp.concatenate([zero, wkb[1::2]], axis=2)], axis=1)
    return (attn_norm.reshape(1, D_MODEL), w1, q_a_norm.reshape(1, Q_RANK), wq,
            kv_a_norm.reshape(1, KV_RANK), wkb, w_pool.astype(BF16),
            pool_scale.reshape(1, POOL_WIDTH))


def kernel(x_prompt, x_sample, mem_prompt, cache_ckv, cache_krope, page_table, state_pool,
           cache_mem_k, cache_mem_v, attn_norm, mem_norm, w_in, q_a_norm, w_q_b, kv_a_norm,
           w_k_b, w_v_b, w_pool, pool_scale, w_mem_kv, w_out, final_norm):
    assert DEPTH == 1 and attn_norm.shape[0] == 1
    n_phys = cache_ckv.shape[1]
    wts = _layout_weights(attn_norm[0], w_in[0], q_a_norm[0], w_q_b[0], kv_a_norm[0], w_k_b[0],
                          w_pool[0], pool_scale[0])
    wvb = jnp.transpose(w_v_b[0], (1, 0, 2)).astype(BF16)
    wout = w_out[0].astype(BF16)
    fn = final_norm.reshape(1, D_MODEL)

    mk, mv, mkb, mvb = _mem_kv(mem_prompt, mem_norm[0].reshape(1, D_MODEL), w_mem_kv[0].astype(BF16))
    cs_p, sn_p = _rope_tables(jnp.arange(SEQ))
    q_p, ckv_p, kr_p, kc_p, pool_p, cross_p, sg_p, pst_p = _proj_prompt(
        x_prompt, cs_p, sn_p, wts, mkb, mvb)
    mla_p = _attn_prompt(q_p, kc_p, wvb)
    rows_p = BATCH * SEQ
    y_p = _combine(x_prompt.reshape(rows_p, D_MODEL), mla_p.reshape(rows_p, MLA_WIDTH),
                   pool_p.reshape(rows_p, POOL_WIDTH), cross_p.reshape(rows_p, MEM_WIDTH),
                   sg_p.reshape(rows_p, D_MIX), wout, fn)

    rows_s = DEC_BATCH * DEC_SEQ
    cs_s, sn_s = _rope_tables(PAST_LEN + jnp.arange(DEC_SEQ))
    xs = x_sample.reshape(rows_s, D_MODEL)
    q_s, ckv_s, kr_s, kc_s, pool_s, qm_s, sg_s, pst_s = _proj_sample(
        xs, cs_s, sn_s, wts, state_pool[0])
    cross_s = _cross_sample(
        qm_s,
        cache_mem_k.reshape(DEC_BATCH, MEM_TOKENS * MEM_HEADS, MEM_HEAD_DIM),
        cache_mem_v.reshape(DEC_BATCH, MEM_TOKENS * MEM_HEADS, MEM_HEAD_DIM))
    o_lat_s = _attn_sample(page_table, q_s, kc_s,
                           cache_ckv.reshape(n_phys, PAGE_SIZE, KV_RANK),
                           jnp.swapaxes(cache_krope.reshape(n_phys, PAGE_SIZE, QK_ROPE_DIM), 1, 2))
    mla_s = _vup_sample(o_lat_s, wvb)
    y_s = _combine(xs, mla_s, pool_s, cross_s, sg_s, wout, fn)

    return (y_p.reshape(BATCH, SEQ, D_MODEL),
            y_s.reshape(DEC_BATCH, DEC_SEQ, D_MODEL),
            ckv_p[None],
            kr_p[None],
            pst_p[None],
            mk.reshape(1, BATCH, MEM_TOKENS, MEM_HEADS, MEM_HEAD_DIM),
            mv.reshape(1, BATCH, MEM_TOKENS, MEM_HEADS, MEM_HEAD_DIM),
            ckv_s.reshape(1, DEC_BATCH, DEC_SEQ, KV_RANK),
            kr_s.reshape(1, DEC_BATCH, DEC_SEQ, QK_ROPE_DIM),
            pst_s[None])
```

```python
import functools

import jax
import jax.numpy as jnp
from jax import lax
from jax.experimental import pallas as pl
from jax.experimental.pallas import tpu as pltpu

F32 = jnp.float32
BF16 = jnp.bfloat16

D_MODEL = 2048
BATCH = 2
SEQ = 4096
DEPTH = 1
DEC_BATCH = 128
DEC_SEQ = 8
PAST_LEN = 16384
PAGE_SIZE = 128
N_PAGES = PAST_LEN // PAGE_SIZE

N_HEADS = 8
QK_NOPE_DIM = 128
QK_ROPE_DIM = 64
V_HEAD_DIM = 128
Q_RANK = 512
KV_RANK = 256
MLA_WIDTH = N_HEADS * V_HEAD_DIM
MLA_SCALE = (QK_NOPE_DIM + QK_ROPE_DIM) ** -0.5
ROPE_THETA = 10000.0
POOL_GROUPS = 4
POOL_WINDOWS = (2, 4, 8, 16)
POOL_GROUP_DIM = 128
POOL_WIDTH = POOL_GROUPS * POOL_GROUP_DIM
POOL_STATE_LEN = 15
MEM_TOKENS = 256
MEM_HEADS = 4
MEM_HEAD_DIM = 128
MEM_WIDTH = MEM_HEADS * MEM_HEAD_DIM
MEM_SCALE = MEM_HEAD_DIM ** -0.5
D_MIX = MLA_WIDTH + POOL_WIDTH + MEM_WIDTH
RMS_EPS = 1e-6

LANE = 128
ROPE_PAD = LANE
QK_WIDTH = KV_RANK + ROPE_PAD
C_QA = 0
C_CKV = C_QA + Q_RANK
C_KR = C_CKV + KV_RANK
C_U = C_KR + 2 * ROPE_PAD
C_QM = C_U + POOL_WIDTH
C_GATE = C_QM + MEM_WIDTH
C_END = C_GATE + D_MIX
Q_NOPE_W = N_HEADS * QK_NOPE_DIM
Q_ROPE_W = N_HEADS * ROPE_PAD

VMEM_LIMIT = 56 * 1024 * 1024
NEG_BIG = -1e30

PROJ_ROWS = 256
GATE_PARTS = 4
ATT_Q = 256
ATT_K = 512
ATT_SUB_HEADS = 1
PAGES_PER_STEP = 32
CROSS_SEQS = 8


def _const_spec(shape):
    n = len(shape)
    return pl.BlockSpec(shape, lambda *_: (0,) * n, pipeline_mode=pl.Buffered(1))


def _params(n_axes):
    return pltpu.CompilerParams(dimension_semantics=("arbitrary",) * n_axes,
                                vmem_limit_bytes=VMEM_LIMIT)


def _rms(x, g):
    return x * lax.rsqrt(jnp.mean(x * x, axis=-1, keepdims=True) + RMS_EPS) * g


def _dot(a, b):
    return jnp.dot(a, b, preferred_element_type=F32)


def _dot_nt(a, b):
    return lax.dot_general(a, b, (((1,), (1,)), ((), ())), preferred_element_type=F32)


def _softmax_rows(s):
    m = jnp.max(s, axis=-1, keepdims=True)
    e = jnp.exp(s - m)
    return e / jnp.sum(e, axis=-1, keepdims=True)


def _mem_kv_kernel(mem_ref, g_ref, w_ref, k_ref, v_ref, kb_ref, vb_ref):
    h = _rms(mem_ref[...], g_ref[...]).astype(BF16)
    kv = _dot(h, w_ref[...])
    k = kv[:, :MEM_WIDTH]
    v = kv[:, MEM_WIDTH:]
    k_ref[...] = k
    v_ref[...] = v
    kb_ref[...] = k.astype(BF16)
    vb_ref[...] = v.astype(BF16)


def _mem_kv(mem, mem_norm, w_mem_kv):
    b = mem.shape[0]
    out_f32 = jax.ShapeDtypeStruct((b, MEM_TOKENS, MEM_WIDTH), F32)
    out_bf16 = jax.ShapeDtypeStruct((b, MEM_TOKENS, MEM_WIDTH), BF16)
    blk = pl.BlockSpec((None, MEM_TOKENS, MEM_WIDTH), lambda i: (i, 0, 0))
    return pl.pallas_call(
        _mem_kv_kernel,
        grid=(b,),
        in_specs=[pl.BlockSpec((None, MEM_TOKENS, D_MODEL), lambda i: (i, 0, 0)),
                  _const_spec((1, D_MODEL)),
                  _const_spec((D_MODEL, 2 * MEM_WIDTH))],
        out_specs=[blk, blk, blk, blk],
        out_shape=[out_f32, out_f32, out_bf16, out_bf16],
        compiler_params=_params(1),
        name="mem_kv",
    )(mem, mem_norm, w_mem_kv)


def _normed_input(x, an_ref):
    return _rms(x, an_ref[...]).astype(BF16)


def _segment(h, w1_ref, a, b):
    for ref, start in zip(w1_ref, (C_QA, C_CKV, C_U)):
        if start <= a and b <= start + ref.shape[1]:
            return _dot(h, ref[:, a - start:b - start])
    raise ValueError(f"columns [{a}, {b}) straddle two weight groups")


def _query_low_rank(h, w1_ref, qan_ref):
    return _rms(_segment(h, w1_ref, C_QA, C_CKV), qan_ref[...]).astype(BF16)


def _query_up(qa, wq_ref):
    q_nope = _dot(qa, wq_ref[:, 0:Q_NOPE_W])
    q_rope = _dot(qa, wq_ref[:, Q_NOPE_W:Q_NOPE_W + Q_ROPE_W])
    q_swap = _dot(qa, wq_ref[:, Q_NOPE_W + Q_ROPE_W:Q_NOPE_W + 2 * Q_ROPE_W])
    return q_nope, q_rope, q_swap


def _query_heads(q_up, cs, sn, wkb_ref):
    q_nope, q_rope, q_swap = q_up
    heads = []
    for pair in range(N_HEADS // 2):
        both = slice(2 * pair * QK_NOPE_DIM, 2 * (pair + 1) * QK_NOPE_DIM)
        q_lat2 = _dot(q_nope[:, both].astype(BF16), wkb_ref[pair]) * MLA_SCALE
        for k in range(2):
            sl = slice((2 * pair + k) * LANE, (2 * pair + k + 1) * LANE)
            q_r = (q_rope[:, sl] * cs + q_swap[:, sl] * sn) * MLA_SCALE
            heads.append((q_lat2[:, k * KV_RANK:(k + 1) * KV_RANK], q_r))
    return heads


def _latent_key(h, cs, sn, w1_ref, kvn_ref):
    z = _segment(h, w1_ref, C_CKV, C_U)
    ckv = _rms(z[:, 0:KV_RANK], kvn_ref[...])
    zk = z[:, KV_RANK:]
    kr = zk[:, :ROPE_PAD] * cs + zk[:, ROPE_PAD:] * sn
    return ckv, kr


def _silu_gate_part(h, w1_ref, sg_ref, part):
    width = D_MIX // GATE_PARTS
    g = _segment(h, w1_ref, C_GATE + part * width, C_GATE + (part + 1) * width)
    sg_ref[:, part * width:(part + 1) * width] = (g / (1.0 + jnp.exp(-g))).astype(BF16)


def _proj_prompt_kernel(x_ref, cs_ref, sn_ref, an_ref, wa_ref, wm_ref, wt_ref, qan_ref, wq_ref,
                        kvn_ref, wkb_ref, wpool_ref, pscale_ref, mk_ref, mv_ref,
                        q_ref, ckv_ref, kr_ref, kc_ref, pool_ref, cross_ref, sg_ref, pst_ref,
                        uext_ref):
    w1_ref = (wa_ref, wm_ref, wt_ref)
    j = pl.program_id(1)
    tm = PROJ_ROWS
    halo = 16

    @pl.when(j == 0)
    def _():
        uext_ref[0:halo, :] = jnp.zeros((halo, POOL_WIDTH), F32)

    cs = cs_ref[...]
    sn = sn_ref[...]
    h = _normed_input(x_ref[...], an_ref)
    u_qm = _segment(h, w1_ref, C_U, C_GATE)
    u = u_qm[:, 0:POOL_WIDTH]
    qm = u_qm[:, POOL_WIDTH:]
    qa = _query_low_rank(h, w1_ref, qan_ref)
    _silu_gate_part(h, w1_ref, sg_ref, 0)

    head_lanes = [slice(hh * MEM_HEAD_DIM, (hh + 1) * MEM_HEAD_DIM) for hh in range(MEM_HEADS)]
    scores = [_dot_nt(qm[:, sl].astype(BF16), mk_ref[:, sl]) for sl in head_lanes]
    probs = [_softmax_rows(s * MEM_SCALE).astype(BF16) for s in scores]
    for sl, p in zip(head_lanes, probs):
        cross_ref[:, sl] = _dot(p, mv_ref[:, sl]).astype(BF16)
    q_up = _query_up(qa, wq_ref)
    _silu_gate_part(h, w1_ref, sg_ref, 1)

    uext_ref[halo:halo + tm, :] = u
    pos = j * tm + lax.broadcasted_iota(jnp.int32, (tm, 1), 0)
    group_lanes = [slice(g * POOL_GROUP_DIM, (g + 1) * POOL_GROUP_DIM) for g in range(POOL_GROUPS)]
    diffs = []
    for sl, w in zip(group_lanes, POOL_WINDOWS):
        acc = u[:, sl]
        for k in range(1, w):
            acc = acc + uext_ref[halo - k:halo - k + tm, sl]
        cnt = jnp.minimum(pos + 1, w).astype(F32)
        diffs.append((acc / cnt - u[:, sl]).astype(BF16))
    for g, (sl, diff) in enumerate(zip(group_lanes, diffs)):
        pool_ref[:, sl] = (_dot(diff, wpool_ref[g]) * pscale_ref[:, sl]).astype(BF16)
    uext_ref[0:halo, :] = uext_ref[tm:tm + halo, :]
    _silu_gate_part(h, w1_ref, sg_ref, 2)

    for hh, (q_lat, q_r) in enumerate(_query_heads(q_up, cs, sn, wkb_ref)):
        q_ref[hh, :, 0:KV_RANK] = q_lat.astype(BF16)
        q_ref[hh, :, KV_RANK:QK_WIDTH] = q_r.astype(BF16)
    _silu_gate_part(h, w1_ref, sg_ref, 3)
    ckv, kr = _latent_key(h, cs, sn, w1_ref, kvn_ref)
    ckv_ref[...] = ckv
    kr_ref[...] = kr[:, :QK_ROPE_DIM]
    kc_ref[:, 0:KV_RANK] = ckv.astype(BF16)
    kc_ref[:, KV_RANK:QK_WIDTH] = kr.astype(BF16)

    @pl.when(j == pl.num_programs(1) - 1)
    def _():
        pst_ref[...] = uext_ref[halo + tm - POOL_STATE_LEN:halo + tm, :]


def _proj_prompt(x, cs, sn, wts, mkb, mvb):
    b, t, _ = x.shape
    tm = PROJ_ROWS
    nj = t // tm

    def rows(width, dtype):
        return (pl.BlockSpec((None, tm, width), lambda i, j: (i, j, 0)),
                jax.ShapeDtypeStruct((b, t, width), dtype))

    outs = [
        (pl.BlockSpec((None, N_HEADS, tm, QK_WIDTH), lambda i, j: (i, 0, j, 0)),
         jax.ShapeDtypeStruct((b, N_HEADS, t, QK_WIDTH), BF16)),
        rows(KV_RANK, F32),
        rows(QK_ROPE_DIM, F32),
        rows(QK_WIDTH, BF16),
        rows(POOL_WIDTH, BF16),
        rows(MEM_WIDTH, BF16),
        rows(D_MIX, BF16),
        (pl.BlockSpec((None, POOL_STATE_LEN, POOL_WIDTH), lambda i, j: (i, 0, 0)),
         jax.ShapeDtypeStruct((b, POOL_STATE_LEN, POOL_WIDTH), F32)),
    ]
    mem_spec = pl.BlockSpec((None, MEM_TOKENS, MEM_WIDTH), lambda i, j: (i, 0, 0))
    in_specs = [
        pl.BlockSpec((None, tm, D_MODEL), lambda i, j: (i, j, 0)),
        pl.BlockSpec((tm, ROPE_PAD), lambda i, j: (j, 0)),
        pl.BlockSpec((tm, ROPE_PAD), lambda i, j: (j, 0)),
    ] + [_const_spec(w.shape) for w in wts] + [mem_spec, mem_spec]
    return pl.pallas_call(
        _proj_prompt_kernel,
        grid=(b, nj),
        in_specs=in_specs,
        out_specs=[o[0] for o in outs],
        out_shape=[o[1] for o in outs],
        scratch_shapes=[pltpu.VMEM((16 + tm, POOL_WIDTH), F32)],
        compiler_params=_params(2),
        name="proj_prompt",
    )(x, cs, sn, *wts, mkb, mvb)


def _attn_prompt_kernel(q_ref, kc_ref, wvb_ref, o_ref, s_even, s_odd, m_ref, l_ref, acc_ref):
    i = pl.program_id(1)
    rows = N_HEADS * ATT_Q
    sub = ATT_SUB_HEADS * ATT_Q
    n_sub = N_HEADS // ATT_SUB_HEADS
    m_ref[...] = jnp.full((rows, 1), NEG_BIG, F32)
    l_ref[...] = jnp.zeros((rows, 1), F32)
    acc_ref[...] = jnp.zeros((rows, KV_RANK), F32)

    def keys(c):
        return kc_ref[pl.ds(pl.multiple_of(c * ATT_K, ATT_K), ATT_K), :]

    def scores(c, s_ref):
        k = keys(c)
        for sb in range(n_sub):
            q = q_ref[sb * ATT_SUB_HEADS:(sb + 1) * ATT_SUB_HEADS].reshape(sub, QK_WIDTH)
            s_ref[sb * sub:(sb + 1) * sub, :] = _dot_nt(q, k)

    def attend(c, s_ref, diag_width=None):
        masked = diag_width is not None
        width = diag_width if masked else ATT_K
        v = kc_ref[pl.ds(pl.multiple_of(c * ATT_K, ATT_K), width), 0:KV_RANK]
        if masked:
            key_pos = c * ATT_K + lax.broadcasted_iota(jnp.int32, (ATT_Q, width), 1)
            q_pos = i * ATT_Q + lax.broadcasted_iota(jnp.int32, (ATT_Q, width), 0)
            visible = (key_pos <= q_pos)[None]
        for sb in range(n_sub):
            r = slice(sb * sub, (sb + 1) * sub)
            s = s_ref[r, 0:width]
            if masked:
                s = jnp.where(visible, s.reshape(ATT_SUB_HEADS, ATT_Q, width), NEG_BIG)
                s = s.reshape(sub, width)
            m_prev = m_ref[r, :]
            m_new = jnp.maximum(m_prev, jnp.max(s, axis=-1, keepdims=True))
            alpha = jnp.exp(m_prev - m_new)
            p = jnp.exp(s - m_new)
            l_ref[r, :] = alpha * l_ref[r, :] + jnp.sum(p, axis=-1, keepdims=True)
            acc_ref[r, :] = alpha * acc_ref[r, :] + _dot(p.astype(BF16), v)
            m_ref[r, :] = m_new

    n_full = (i * ATT_Q) // ATT_K
    scores(0, s_even)

    def body(c, carry):
        @pl.when(lax.rem(c, 2) == 0)
        def _():
            scores(c + 1, s_odd)
            attend(c, s_even)

        @pl.when(lax.rem(c, 2) == 1)
        def _():
            scores(c + 1, s_even)
            attend(c, s_odd)

        return carry

    lax.fori_loop(0, n_full, body, 0)

    blocks_per_chunk = ATT_K // ATT_Q
    for parity, s_ref in enumerate((s_even, s_odd)):
        for d in range(blocks_per_chunk):
            @pl.when(jnp.logical_and(lax.rem(n_full, 2) == parity, lax.rem(i, blocks_per_chunk) == d))
            def _(s_ref=s_ref, d=d):
                attend(n_full, s_ref, diag_width=(d + 1) * ATT_Q)

    for hh in range(N_HEADS):
        r = slice(hh * ATT_Q, (hh + 1) * ATT_Q)
        o_h = (acc_ref[r, :] / l_ref[r, :]).astype(BF16)
        o_ref[:, hh * V_HEAD_DIM:(hh + 1) * V_HEAD_DIM] = _dot(o_h, wvb_ref[hh]).astype(BF16)


def _attn_prompt(q, kc, wvb):
    b, _, t, _ = q.shape
    rows = N_HEADS * ATT_Q
    return pl.pallas_call(
        _attn_prompt_kernel,
        grid=(b, t // ATT_Q),
        in_specs=[pl.BlockSpec((None, N_HEADS, ATT_Q, QK_WIDTH), lambda bi, i: (bi, 0, i, 0)),
                  pl.BlockSpec((None, t, QK_WIDTH), lambda bi, i: (bi, 0, 0)),
                  _const_spec(wvb.shape)],
        out_specs=pl.BlockSpec((None, ATT_Q, MLA_WIDTH), lambda bi, i: (bi, i, 0)),
        out_shape=jax.ShapeDtypeStruct((b, t, MLA_WIDTH), BF16),
        scratch_shapes=[pltpu.VMEM((rows, ATT_K), F32), pltpu.VMEM((rows, ATT_K), F32),
                        pltpu.VMEM((rows, 1), F32), pltpu.VMEM((rows, 1), F32),
                        pltpu.VMEM((rows, KV_RANK), F32)],
        compiler_params=_params(2),
        name="attn_prompt",
    )(q, kc, wvb)


def _combine_kernel(x_ref, mla_ref, pool_ref, cross_ref, sg_ref, wout_ref, fn_ref, y_ref):
    def part(ref, a, b):
        mix = ref[...].astype(F32) * sg_ref[:, a:b].astype(F32)
        return _dot(mix.astype(BF16), wout_ref[a:b, :])

    acc = x_ref[...] + part(mla_ref, 0, MLA_WIDTH)
    acc = acc + part(pool_ref, MLA_WIDTH, MLA_WIDTH + POOL_WIDTH)
    acc = acc + part(cross_ref, MLA_WIDTH + POOL_WIDTH, D_MIX)
    y_ref[...] = _rms(acc, fn_ref[...])


def _combine(x, mla, pool, cross, sg, wout, fn):
    n = x.shape[0]
    tm = PROJ_ROWS

    def rows(width):
        return pl.BlockSpec((tm, width), lambda i: (i, 0))

    return pl.pallas_call(
        _combine_kernel,
        grid=(n // tm,),
        in_specs=[rows(D_MODEL), rows(MLA_WIDTH), rows(POOL_WIDTH), rows(MEM_WIDTH), rows(D_MIX),
                  _const_spec(wout.shape), _const_spec(fn.shape)],
        out_specs=rows(D_MODEL),
        out_shape=jax.ShapeDtypeStruct((n, D_MODEL), F32),
        compiler_params=_params(1),
        name="combine",
    )(x, mla, pool, cross, sg, wout, fn)


def _proj_sample_kernel(x_ref, cs_ref, sn_ref, an_ref, wa_ref, wm_ref, wt_ref, qan_ref, wq_ref,
                        kvn_ref, wkb_ref, wpool_ref, pscale_ref, st_ref,
                        q_ref, ckv_ref, kr_ref, kc_ref, pool_ref, qm_ref, sg_ref, pst_ref,
                        ext_ref):
    w1_ref = (wa_ref, wm_ref, wt_ref)
    tm = PROJ_ROWS
    ns = tm // DEC_SEQ
    cs = jnp.broadcast_to(cs_ref[...][None], (ns, DEC_SEQ, ROPE_PAD)).reshape(tm, ROPE_PAD)
    sn = jnp.broadcast_to(sn_ref[...][None], (ns, DEC_SEQ, ROPE_PAD)).reshape(tm, ROPE_PAD)
    h = _normed_input(x_ref[...], an_ref)
    u_qm = _segment(h, w1_ref, C_U, C_GATE)
    u = u_qm[:, 0:POOL_WIDTH]
    qm_ref[...] = u_qm[:, POOL_WIDTH:]
    q_up = _query_up(_query_low_rank(h, w1_ref, qan_ref), wq_ref)
    for hh, (q_lat, q_r) in enumerate(_query_heads(q_up, cs, sn, wkb_ref)):
        q_ref[:, hh, :, 0:KV_RANK] = q_lat.reshape(ns, DEC_SEQ, KV_RANK)
        q_ref[:, hh, :, KV_RANK:QK_WIDTH] = q_r.reshape(ns, DEC_SEQ, ROPE_PAD)
    ckv, kr = _latent_key(h, cs, sn, w1_ref, kvn_ref)
    ckv_ref[...] = ckv
    kr_ref[...] = kr[:, :QK_ROPE_DIM]
    kc_ref[:, 0:KV_RANK] = ckv
    kc_ref[:, KV_RANK:QK_WIDTH] = kr
    for part in range(GATE_PARTS):
        _silu_gate_part(h, w1_ref, sg_ref, part)

    base = POOL_STATE_LEN + 1
    ext_ref[:, 1:base, :] = st_ref[...]
    ext_ref[:, base:base + DEC_SEQ, :] = u.reshape(ns, DEC_SEQ, POOL_WIDTH)
    for g, w in enumerate(POOL_WINDOWS):
        sl = slice(g * POOL_GROUP_DIM, (g + 1) * POOL_GROUP_DIM)
        acc = ext_ref[:, base:base + DEC_SEQ, sl]
        for k in range(1, w):
            acc = acc + ext_ref[:, base - k:base - k + DEC_SEQ, sl]
        diff = (acc * (1.0 / w)).reshape(tm, POOL_GROUP_DIM) - u[:, sl]
        y = _dot(diff.astype(BF16), wpool_ref[g]) * pscale_ref[:, sl]
        pool_ref[:, sl] = y.astype(BF16)
    pst_ref[...] = ext_ref[:, base + DEC_SEQ - POOL_STATE_LEN:base + DEC_SEQ, :]


def _proj_sample(x, cs, sn, wts, state):
    n = x.shape[0]
    tm = PROJ_ROWS
    ns = tm // DEC_SEQ
    nseq = n // DEC_SEQ

    def rows(width, dtype):
        return (pl.BlockSpec((tm, width), lambda i: (i, 0)), jax.ShapeDtypeStruct((n, width), dtype))

    outs = [
        (pl.BlockSpec((ns, N_HEADS, DEC_SEQ, QK_WIDTH), lambda i: (i, 0, 0, 0)),
         jax.ShapeDtypeStruct((nseq, N_HEADS, DEC_SEQ, QK_WIDTH), F32)),
        rows(KV_RANK, F32),
        rows(QK_ROPE_DIM, F32),
        rows(QK_WIDTH, F32),
        rows(POOL_WIDTH, BF16),
        rows(MEM_WIDTH, F32),
        rows(D_MIX, BF16),
        (pl.BlockSpec((ns, POOL_STATE_LEN, POOL_WIDTH), lambda i: (i, 0, 0)),
         jax.ShapeDtypeStruct((nseq, POOL_STATE_LEN, POOL_WIDTH), F32)),
    ]
    in_specs = [
        pl.BlockSpec((tm, D_MODEL), lambda i: (i, 0)),
        _const_spec(cs.shape),
        _const_spec(sn.shape),
    ] + [_const_spec(w.shape) for w in wts] + [
        pl.BlockSpec((ns, POOL_STATE_LEN, POOL_WIDTH), lambda i: (i, 0, 0))]
    return pl.pallas_call(
        _proj_sample_kernel,
        grid=(n // tm,),
        in_specs=in_specs,
        out_specs=[o[0] for o in outs],
        out_shape=[o[1] for o in outs],
        scratch_shapes=[pltpu.VMEM((ns, POOL_STATE_LEN + 1 + DEC_SEQ, POOL_WIDTH), F32)],
        compiler_params=_params(1),
        name="proj_sample",
    )(x, cs, sn, *wts, state)


def _cross_sample_kernel(qm_ref, k_ref, v_ref, o_ref):
    rows = MEM_HEADS * DEC_SEQ
    cols = MEM_TOKENS * MEM_HEADS
    r_head = lax.broadcasted_iota(jnp.int32, (rows, cols), 0) // DEC_SEQ
    c_head = lax.broadcasted_iota(jnp.int32, (rows, cols), 1) % MEM_HEADS
    same_head = r_head == c_head
    scores = []
    for s_i in range(CROSS_SEQS):
        q = qm_ref[s_i * DEC_SEQ:(s_i + 1) * DEC_SEQ, :]
        q_ht = jnp.concatenate([q[:, hh * MEM_HEAD_DIM:(hh + 1) * MEM_HEAD_DIM]
                                for hh in range(MEM_HEADS)], axis=0).astype(BF16)
        scores.append(_dot_nt(q_ht, k_ref[s_i].astype(BF16)))
    probs = [_softmax_rows(jnp.where(same_head, s * MEM_SCALE, NEG_BIG)).astype(BF16) for s in scores]
    outs = []
    for s_i in range(CROSS_SEQS):
        o_ht = _dot(probs[s_i], v_ref[s_i].astype(BF16))
        outs.append(jnp.concatenate([o_ht[hh * DEC_SEQ:(hh + 1) * DEC_SEQ]
                                     for hh in range(MEM_HEADS)], axis=-1))
    o_ref[...] = jnp.concatenate(outs, axis=0).astype(BF16)


def _cross_sample(qm, mem_k, mem_v):
    n = qm.shape[0]
    tm = CROSS_SEQS * DEC_SEQ
    kv_spec = pl.BlockSpec((CROSS_SEQS, MEM_TOKENS * MEM_HEADS, MEM_HEAD_DIM), lambda i: (i, 0, 0))
    return pl.pallas_call(
        _cross_sample_kernel,
        grid=(n // tm,),
        in_specs=[pl.BlockSpec((tm, MEM_WIDTH), lambda i: (i, 0)), kv_spec, kv_spec],
        out_specs=pl.BlockSpec((tm, MEM_WIDTH), lambda i: (i, 0)),
        out_shape=jax.ShapeDtypeStruct((n, MEM_WIDTH), BF16),
        compiler_params=_params(1),
        name="cross_sample",
    )(qm, mem_k, mem_v)


def _page_copies(pages, ckv_hbm, krt_hbm, ckv_buf, krt_buf, sem, slot):
    copies = []
    for j, pg in enumerate(pages):
        copies.append(pltpu.make_async_copy(
            ckv_hbm.at[pg], ckv_buf.at[slot, pl.ds(j * PAGE_SIZE, PAGE_SIZE), :], sem.at[slot]))
        copies.append(pltpu.make_async_copy(krt_hbm.at[pg], krt_buf.at[slot, j], sem.at[slot]))
    return copies


def _attn_sample_kernel(pt_ref, q_ref, qn_ref, knew_ref, ckv_hbm, krt_hbm, o_ref,
                        ckv_buf, krt_buf, sem, s_buf, kcb_buf, m_ref, l_ref, acc_ref):
    npg = PAGES_PER_STEP
    nch = N_PAGES // npg
    s_i = pl.program_id(0)
    c = pl.program_id(1)
    step = s_i * nch + c
    last_step = pl.num_programs(0) * nch - 1
    parity = lax.rem(step, 2)
    rows = N_HEADS * DEC_SEQ

    def advance(seq, chunk):
        wrap = chunk + 1 == nch
        return jnp.where(wrap, seq + 1, seq), jnp.where(wrap, 0, chunk + 1)

    def start_fetch(seq, chunk, slot):
        pages = [pt_ref[seq, chunk * npg + j] for j in range(npg)]
        for n, cp in enumerate(_page_copies(pages, ckv_hbm, krt_hbm, ckv_buf, krt_buf, sem, slot)):
            cp.start(priority=(n // 2) % 2)

    def wait(slot):
        for cp in _page_copies([0] * npg, ckv_hbm, krt_hbm, ckv_buf, krt_buf, sem, slot):
            cp.wait()

    def scores(q_rows, p):
        kc = ckv_buf[p].astype(BF16)
        krt = jnp.concatenate([krt_buf[p, j] for j in range(npg)], axis=-1).astype(BF16)
        kcb_buf[p] = kc
        s_buf[p] = (_dot_nt(q_rows[:, 0:KV_RANK], kc)
                    + _dot(q_rows[:, KV_RANK:KV_RANK + QK_ROPE_DIM], krt))

    seq1, chunk1 = advance(s_i, c)
    seq2, chunk2 = advance(seq1, chunk1)
    q_f32 = q_ref[...].reshape(rows, QK_WIDTH)
    q = q_f32.astype(BF16)

    @pl.when(step == 0)
    def _():
        start_fetch(s_i, c, 0)
        start_fetch(seq1, chunk1, 1)
        wait(0)
        scores(q, 0)

    @pl.when(step + 2 <= last_step)
    def _():
        start_fetch(seq2, chunk2, parity)

    @pl.when(step < last_step)
    def _():
        wait(1 - parity)

    @pl.when(c == 0)
    def _():
        pad = jnp.zeros((PAGE_SIZE - DEC_SEQ, QK_WIDTH), F32)
        kn = jnp.concatenate([knew_ref[...], pad], axis=0).astype(BF16)
        s = _dot_nt(q, kn)
        key_j = lax.broadcasted_iota(jnp.int32, (rows, PAGE_SIZE), 1)
        q_t = lax.broadcasted_iota(jnp.int32, (rows, PAGE_SIZE), 0) % DEC_SEQ
        s = jnp.where(key_j <= q_t, s, NEG_BIG)
        m = jnp.max(s, axis=-1, keepdims=True)
        p = jnp.exp(s - m)
        m_ref[...] = m
        l_ref[...] = jnp.sum(p, axis=-1, keepdims=True)
        acc_ref[...] = _dot(p.astype(BF16), kn[:, 0:KV_RANK])

    q_next = jnp.where(c == nch - 1, qn_ref[...].reshape(rows, QK_WIDTH), q_f32).astype(BF16)

    def attend_and_score_next(p):
        scores(q_next, 1 - p)
        s = s_buf[p]
        m_prev = m_ref[...]
        m_new = jnp.maximum(m_prev, jnp.max(s, axis=-1, keepdims=True))
        alpha = jnp.exp(m_prev - m_new)
        prob = jnp.exp(s - m_new)
        l_ref[...] = alpha * l_ref[...] + jnp.sum(prob, axis=-1, keepdims=True)
        acc_ref[...] = alpha * acc_ref[...] + _dot(prob.astype(BF16), kcb_buf[p])
        m_ref[...] = m_new

    @pl.when(parity == 0)
    def _():
        attend_and_score_next(0)

    @pl.when(parity == 1)
    def _():
        attend_and_score_next(1)

    @pl.when(c == nch - 1)
    def _():
        o_ref[...] = (acc_ref[...] / l_ref[...]).reshape(N_HEADS, DEC_SEQ, KV_RANK)


def _attn_sample(page_table, q, knew, cache_ckv, cache_krope_t):
    nseq = q.shape[0]
    npg = PAGES_PER_STEP
    rows = N_HEADS * DEC_SEQ
    keys = npg * PAGE_SIZE
    q_block = (None, N_HEADS, DEC_SEQ, QK_WIDTH)
    grid_spec = pltpu.PrefetchScalarGridSpec(
        num_scalar_prefetch=1,
        grid=(nseq, N_PAGES // npg),
        in_specs=[pl.BlockSpec(q_block, lambda s, c, pt: (s, 0, 0, 0)),
                  pl.BlockSpec(q_block, lambda s, c, pt: (jnp.minimum(s + 1, nseq - 1), 0, 0, 0)),
                  pl.BlockSpec((DEC_SEQ, QK_WIDTH), lambda s, c, pt: (s, 0)),
                  pl.BlockSpec(memory_space=pl.ANY),
                  pl.BlockSpec(memory_space=pl.ANY)],
        out_specs=pl.BlockSpec((None, N_HEADS, DEC_SEQ, KV_RANK), lambda s, c, pt: (s, 0, 0, 0)),
        scratch_shapes=[pltpu.VMEM((2, keys, KV_RANK), F32),
                        pltpu.VMEM((2, npg, QK_ROPE_DIM, PAGE_SIZE), F32),
                        pltpu.SemaphoreType.DMA((2,)),
                        pltpu.VMEM((2, rows, keys), F32),
                        pltpu.VMEM((2, keys, KV_RANK), BF16),
                        pltpu.VMEM((rows, 1), F32), pltpu.VMEM((rows, 1), F32),
                        pltpu.VMEM((rows, KV_RANK), F32)],
    )
    return pl.pallas_call(
        _attn_sample_kernel,
        grid_spec=grid_spec,
        out_shape=jax.ShapeDtypeStruct((nseq, N_HEADS, DEC_SEQ, KV_RANK), F32),
        compiler_params=_params(2),
        name="attn_sample",
    )(page_table, q, q, knew, cache_ckv, cache_krope_t)


def _vup_sample_kernel(o_ref, wvb_ref, mla_ref):
    ns = PROJ_ROWS // DEC_SEQ
    for hh in range(N_HEADS):
        o_h = o_ref[:, hh, :, :].reshape(ns * DEC_SEQ, KV_RANK).astype(BF16)
        mla_ref[:, hh * V_HEAD_DIM:(hh + 1) * V_HEAD_DIM] = _dot(o_h, wvb_ref[hh]).astype(BF16)


def _vup_sample(o_lat, wvb):
    nseq = o_lat.shape[0]
    ns = PROJ_ROWS // DEC_SEQ
    return pl.pallas_call(
        _vup_sample_kernel,
        grid=(nseq // ns,),
        in_specs=[pl.BlockSpec((ns, N_HEADS, DEC_SEQ, KV_RANK), lambda i: (i, 0, 0, 0)),
                  _const_spec(wvb.shape)],
        out_specs=pl.BlockSpec((PROJ_ROWS, MLA_WIDTH), lambda i: (i, 0)),
        out_shape=jax.ShapeDtypeStruct((nseq * DEC_SEQ, MLA_WIDTH), BF16),
        compiler_params=_params(1),
        name="vup_sample",
    )(o_lat, wvb)


def _rope_tables(pos):
    inv = ROPE_THETA ** (-jnp.arange(0, QK_ROPE_DIM, 2, dtype=F32) / QK_ROPE_DIM)
    ang = pos.astype(F32)[:, None] * inv[None, :]
    cos, sin = jnp.cos(ang), jnp.sin(ang)
    reps = ROPE_PAD // QK_ROPE_DIM
    cs = jnp.tile(jnp.concatenate([cos, cos], axis=-1), (1, reps))
    sn = jnp.tile(jnp.concatenate([-sin, sin], axis=-1), (1, reps))
    return cs, sn


def _swap_halves(w):
    half = w.shape[-1] // 2
    return jnp.concatenate([w[..., half:], w[..., :half]], axis=-1)


def _pad_lanes(w, width):
    return jnp.pad(w, [(0, 0)] * (w.ndim - 1) + [(0, width - w.shape[-1])])


def _layout_weights(attn_norm, w_in, q_a_norm, w_q_b, kv_a_norm, w_k_b, w_pool, pool_scale):
    s0 = Q_RANK
    s1 = s0 + KV_RANK
    s2 = s1 + QK_ROPE_DIM
    w_kr = w_in[:, s1:s2]
    w_a = w_in[:, :s0].astype(BF16)
    w_m = jnp.concatenate([w_in[:, s0:s1], _pad_lanes(w_kr, ROPE_PAD),
                           _pad_lanes(_swap_halves(w_kr), ROPE_PAD)], axis=1).astype(BF16)
    w_t = w_in[:, s2:].astype(BF16)
    wq_nope = w_q_b[:, :, :QK_NOPE_DIM].reshape(Q_RANK, Q_NOPE_W)
    wq_r = w_q_b[:, :, QK_NOPE_DIM:]
    wq_rope = _pad_lanes(wq_r, ROPE_PAD).reshape(Q_RANK, Q_ROPE_W)
    wq_swap = _pad_lanes(_swap_halves(wq_r), ROPE_PAD).reshape(Q_RANK, Q_ROPE_W)
    wq = jnp.concatenate([wq_nope, wq_rope, wq_swap], axis=1).astype(BF16)
    wkb = jnp.transpose(w_k_b, (1, 2, 0)).astype(BF16)
    zero = jnp.zeros_like(wkb[0::2])
    wkb = jnp.concatenate([jnp.concatenate([wkb[0::2], zero], axis=2),
                           jnp.concatenate([zero, wkb[1::2]], axis=2)], axis=1)
    return (attn_norm.reshape(1, D_MODEL), w_a, w_m, w_t, q_a_norm.reshape(1, Q_RANK), wq,
            kv_a_norm.reshape(1, KV_RANK), wkb, w_pool.astype(BF16),
            pool_scale.reshape(1, POOL_WIDTH))


def kernel(x_prompt, x_sample, mem_prompt, cache_ckv, cache_krope, page_table, state_pool,
           cache_mem_k, cache_mem_v, attn_norm, mem_norm, w_in, q_a_norm, w_q_b, kv_a_norm,
           w_k_b, w_v_b, w_pool, pool_scale, w_mem_kv, w_out, final_norm):
    assert DEPTH == 1 and attn_norm.shape[0] == 1
    n_phys = cache_ckv.shape[1]
    wts = _layout_weights(attn_norm[0], w_in[0], q_a_norm[0], w_q_b[0], kv_a_norm[0], w_k_b[0],
                          w_pool[0], pool_scale[0])
    wvb = jnp.transpose(w_v_b[0], (1, 0, 2)).astype(BF16)
    wout = w_out[0].astype(BF16)
    fn = final_norm.reshape(1, D_MODEL)

    mk, mv, mkb, mvb = _mem_kv(mem_prompt, mem_norm[0].reshape(1, D_MODEL), w_mem_kv[0].astype(BF16))
    cs_p, sn_p = _rope_tables(jnp.arange(SEQ))
    q_p, ckv_p, kr_p, kc_p, pool_p, cross_p, sg_p, pst_p = _proj_prompt(
        x_prompt, cs_p, sn_p, wts, mkb, mvb)
    mla_p = _attn_prompt(q_p, kc_p, wvb)
    rows_p = BATCH * SEQ
    y_p = _combine(x_prompt.reshape(rows_p, D_MODEL), mla_p.reshape(rows_p, MLA_WIDTH),
                   pool_p.reshape(rows_p, POOL_WIDTH), cross_p.reshape(rows_p, MEM_WIDTH),
                   sg_p.reshape(rows_p, D_MIX), wout, fn)

    rows_s = DEC_BATCH * DEC_SEQ
    cs_s, sn_s = _rope_tables(PAST_LEN + jnp.arange(DEC_SEQ))
    xs = x_sample.reshape(rows_s, D_MODEL)
    q_s, ckv_s, kr_s, kc_s, pool_s, qm_s, sg_s, pst_s = _proj_sample(
        xs, cs_s, sn_s, wts, state_pool[0])
    cross_s = _cross_sample(
        qm_s,
        cache_mem_k.reshape(DEC_BATCH, MEM_TOKENS * MEM_HEADS, MEM_HEAD_DIM),
        cache_mem_v.reshape(DEC_BATCH, MEM_TOKENS * MEM_HEADS, MEM_HEAD_DIM))
    o_lat_s = _attn_sample(page_table, q_s, kc_s,
                           cache_ckv.reshape(n_phys, PAGE_SIZE, KV_RANK),
                           jnp.swapaxes(cache_krope.reshape(n_phys, PAGE_SIZE, QK_ROPE_DIM), 1, 2))
    mla_s = _vup_sample(o_lat_s, wvb)
    y_s = _combine(xs, mla_s, pool_s, cross_s, sg_s, wout, fn)

    return (y_p.reshape(BATCH, SEQ, D_MODEL),
            y_s.reshape(DEC_BATCH, DEC_SEQ, D_MODEL),
            ckv_p[None],
            kr_p[None],
            pst_p[None],
            mk.reshape(1, BATCH, MEM_TOKENS, MEM_HEADS, MEM_HEAD_DIM),
            mv.reshape(1, BATCH, MEM_TOKENS, MEM_HEADS, MEM_HEAD_DIM),
            ckv_s.reshape(1, DEC_BATCH, DEC_SEQ, KV_RANK),
            kr_s.reshape(1, DEC_BATCH, DEC_SEQ, QK_ROPE_DIM),
            pst_s[None])
```

```python
import functools

import jax
import jax.numpy as jnp
from jax import lax
from jax.experimental import pallas as pl
from jax.experimental.pallas import tpu as pltpu

F32 = jnp.float32
BF16 = jnp.bfloat16

D_MODEL = 2048
BATCH = 2
SEQ = 4096
DEPTH = 1
DEC_BATCH = 128
DEC_SEQ = 8
PAST_LEN = 16384
PAGE_SIZE = 128
N_PAGES = PAST_LEN // PAGE_SIZE

N_HEADS = 8
QK_NOPE_DIM = 128
QK_ROPE_DIM = 64
V_HEAD_DIM = 128
Q_RANK = 512
KV_RANK = 256
MLA_WIDTH = N_HEADS * V_HEAD_DIM
MLA_SCALE = (QK_NOPE_DIM + QK_ROPE_DIM) ** -0.5
ROPE_THETA = 10000.0
POOL_GROUPS = 4
POOL_WINDOWS = (2, 4, 8, 16)
POOL_GROUP_DIM = 128
POOL_WIDTH = POOL_GROUPS * POOL_GROUP_DIM
POOL_STATE_LEN = 15
MEM_TOKENS = 256
MEM_HEADS = 4
MEM_HEAD_DIM = 128
MEM_WIDTH = MEM_HEADS * MEM_HEAD_DIM
MEM_SCALE = MEM_HEAD_DIM ** -0.5
D_MIX = MLA_WIDTH + POOL_WIDTH + MEM_WIDTH
RMS_EPS = 1e-6

LANE = 128
ROPE_PAD = LANE
QK_WIDTH = KV_RANK + ROPE_PAD
C_QA = 0
C_CKV = C_QA + Q_RANK
C_KR = C_CKV + KV_RANK
C_U = C_KR + 2 * ROPE_PAD
C_QM = C_U + POOL_WIDTH
C_GATE = C_QM + MEM_WIDTH
C_END = C_GATE + D_MIX
Q_NOPE_W = N_HEADS * QK_NOPE_DIM
Q_ROPE_W = N_HEADS * ROPE_PAD

VMEM_LIMIT = 56 * 1024 * 1024
NEG_BIG = -1e30

PROJ_ROWS = 256
GATE_PARTS = 4
ATT_Q = 256
ATT_K = 512
ATT_SUB_HEADS = 1
PAGES_PER_STEP = 32
CROSS_SEQS = 8


def _const_spec(shape):
    n = len(shape)
    return pl.BlockSpec(shape, lambda *_: (0,) * n, pipeline_mode=pl.Buffered(1))


def _params(n_axes):
    return pltpu.CompilerParams(dimension_semantics=("arbitrary",) * n_axes,
                                vmem_limit_bytes=VMEM_LIMIT)


def _rms(x, g):
    return x * lax.rsqrt(jnp.mean(x * x, axis=-1, keepdims=True) + RMS_EPS) * g


def _dot(a, b):
    return jnp.dot(a, b, preferred_element_type=F32)


def _dot_nt(a, b):
    return lax.dot_general(a, b, (((1,), (1,)), ((), ())), preferred_element_type=F32)


def _softmax_rows(s):
    m = jnp.max(s, axis=-1, keepdims=True)
    e = jnp.exp(s - m)
    return e / jnp.sum(e, axis=-1, keepdims=True)


def _mem_kv_kernel(mem_ref, g_ref, w_ref, k_ref, v_ref, kb_ref, vb_ref):
    h = _rms(mem_ref[...], g_ref[...]).astype(BF16)
    kv = _dot(h, w_ref[...])
    k = kv[:, :MEM_WIDTH]
    v = kv[:, MEM_WIDTH:]
    k_ref[...] = k
    v_ref[...] = v
    kb_ref[...] = k.astype(BF16)
    vb_ref[...] = v.astype(BF16)


def _mem_kv(mem, mem_norm, w_mem_kv):
    b = mem.shape[0]
    out_f32 = jax.ShapeDtypeStruct((b, MEM_TOKENS, MEM_WIDTH), F32)
    out_bf16 = jax.ShapeDtypeStruct((b, MEM_TOKENS, MEM_WIDTH), BF16)
    blk = pl.BlockSpec((None, MEM_TOKENS, MEM_WIDTH), lambda i: (i, 0, 0))
    return pl.pallas_call(
        _mem_kv_kernel,
        grid=(b,),
        in_specs=[pl.BlockSpec((None, MEM_TOKENS, D_MODEL), lambda i: (i, 0, 0)),
                  _const_spec((1, D_MODEL)),
                  _const_spec((D_MODEL, 2 * MEM_WIDTH))],
        out_specs=[blk, blk, blk, blk],
        out_shape=[out_f32, out_f32, out_bf16, out_bf16],
        compiler_params=_params(1),
        name="mem_kv",
    )(mem, mem_norm, w_mem_kv)


def _normed_input(x, an_ref):
    return _rms(x, an_ref[...]).astype(BF16)


def _segment(h, w1_ref, a, b):
    for ref, start in zip(w1_ref, (C_QA, C_CKV, C_U)):
        if start <= a and b <= start + ref.shape[1]:
            return _dot(h, ref[:, a - start:b - start])
    raise ValueError(f"columns [{a}, {b}) straddle two weight groups")


def _query_low_rank(h, w1_ref, qan_ref):
    return _rms(_segment(h, w1_ref, C_QA, C_CKV), qan_ref[...]).astype(BF16)


def _query_up(qa, wq_ref):
    q_nope = _dot(qa, wq_ref[:, 0:Q_NOPE_W])
    q_rope = _dot(qa, wq_ref[:, Q_NOPE_W:Q_NOPE_W + Q_ROPE_W])
    q_swap = _dot(qa, wq_ref[:, Q_NOPE_W + Q_ROPE_W:Q_NOPE_W + 2 * Q_ROPE_W])
    return q_nope, q_rope, q_swap


def _query_heads(q_up, cs, sn, wkb_ref):
    q_nope, q_rope, q_swap = q_up
    heads = []
    for pair in range(N_HEADS // 2):
        both = slice(2 * pair * QK_NOPE_DIM, 2 * (pair + 1) * QK_NOPE_DIM)
        q_lat2 = _dot(q_nope[:, both].astype(BF16), wkb_ref[pair]) * MLA_SCALE
        for k in range(2):
            sl = slice((2 * pair + k) * LANE, (2 * pair + k + 1) * LANE)
            q_r = (q_rope[:, sl] * cs + q_swap[:, sl] * sn) * MLA_SCALE
            heads.append((q_lat2[:, k * KV_RANK:(k + 1) * KV_RANK], q_r))
    return heads


def _latent_key(h, cs, sn, w1_ref, kvn_ref):
    z = _segment(h, w1_ref, C_CKV, C_U)
    ckv = _rms(z[:, 0:KV_RANK], kvn_ref[...])
    zk = z[:, KV_RANK:]
    kr = zk[:, :ROPE_PAD] * cs + zk[:, ROPE_PAD:] * sn
    return ckv, kr


def _silu_gate_part(h, w1_ref, sg_ref, part):
    width = D_MIX // GATE_PARTS
    g = _segment(h, w1_ref, C_GATE + part * width, C_GATE + (part + 1) * width)
    sg_ref[:, part * width:(part + 1) * width] = (g / (1.0 + jnp.exp(-g))).astype(BF16)


def _proj_prompt_kernel(x_ref, cs_ref, sn_ref, an_ref, wa_ref, wm_ref, wt_ref, qan_ref, wq_ref,
                        kvn_ref, wkb_ref, wpool_ref, pscale_ref, mk_ref, mv_ref,
                        q_ref, ckv_ref, kr_ref, kc_ref, pool_ref, cross_ref, sg_ref, pst_ref,
                        uext_ref):
    w1_ref = (wa_ref, wm_ref, wt_ref)
    j = pl.program_id(1)
    tm = PROJ_ROWS
    halo = 16

    @pl.when(j == 0)
    def _():
        uext_ref[0:halo, :] = jnp.zeros((halo, POOL_WIDTH), F32)

    cs = cs_ref[...]
    sn = sn_ref[...]
    h = _normed_input(x_ref[...], an_ref)
    u_qm = _segment(h, w1_ref, C_U, C_GATE)
    u = u_qm[:, 0:POOL_WIDTH]
    qm = u_qm[:, POOL_WIDTH:]
    qa = _query_low_rank(h, w1_ref, qan_ref)
    _silu_gate_part(h, w1_ref, sg_ref, 0)

    head_lanes = [slice(hh * MEM_HEAD_DIM, (hh + 1) * MEM_HEAD_DIM) for hh in range(MEM_HEADS)]
    scores = [_dot_nt(qm[:, sl].astype(BF16), mk_ref[:, sl]) for sl in head_lanes]
    probs = [_softmax_rows(s * MEM_SCALE).astype(BF16) for s in scores]
    for sl, p in zip(head_lanes, probs):
        cross_ref[:, sl] = _dot(p, mv_ref[:, sl]).astype(BF16)
    q_up = _query_up(qa, wq_ref)
    _silu_gate_part(h, w1_ref, sg_ref, 1)

    uext_ref[halo:halo + tm, :] = u
    pos = j * tm + lax.broadcasted_iota(jnp.int32, (tm, 1), 0)
    group_lanes = [slice(g * POOL_GROUP_DIM, (g + 1) * POOL_GROUP_DIM) for g in range(POOL_GROUPS)]
    diffs = []
    for sl, w in zip(group_lanes, POOL_WINDOWS):
        acc = u[:, sl]
        for k in range(1, w):
            acc = acc + uext_ref[halo - k:halo - k + tm, sl]
        cnt = jnp.minimum(pos + 1, w).astype(F32)
        diffs.append((acc / cnt - u[:, sl]).astype(BF16))
    for g, (sl, diff) in enumerate(zip(group_lanes, diffs)):
        pool_ref[:, sl] = (_dot(diff, wpool_ref[g]) * pscale_ref[:, sl]).astype(BF16)
    uext_ref[0:halo, :] = uext_ref[tm:tm + halo, :]
    _silu_gate_part(h, w1_ref, sg_ref, 2)

    for hh, (q_lat, q_r) in enumerate(_query_heads(q_up, cs, sn, wkb_ref)):
        q_ref[hh, :, 0:KV_RANK] = q_lat.astype(BF16)
        q_ref[hh, :, KV_RANK:QK_WIDTH] = q_r.astype(BF16)
    _silu_gate_part(h, w1_ref, sg_ref, 3)
    ckv, kr = _latent_key(h, cs, sn, w1_ref, kvn_ref)
    ckv_ref[...] = ckv
    kr_ref[...] = kr[:, :QK_ROPE_DIM]
    kc_ref[:, 0:KV_RANK] = ckv.astype(BF16)
    kc_ref[:, KV_RANK:QK_WIDTH] = kr.astype(BF16)

    @pl.when(j == pl.num_programs(1) - 1)
    def _():
        pst_ref[...] = uext_ref[halo + tm - POOL_STATE_LEN:halo + tm, :]


def _proj_prompt(x, cs, sn, wts, mkb, mvb):
    b, t, _ = x.shape
    tm = PROJ_ROWS
    nj = t // tm

    def rows(width, dtype):
        return (pl.BlockSpec((None, tm, width), lambda i, j: (i, j, 0)),
                jax.ShapeDtypeStruct((b, t, width), dtype))

    outs = [
        (pl.BlockSpec((None, N_HEADS, tm, QK_WIDTH), lambda i, j: (i, 0, j, 0)),
         jax.ShapeDtypeStruct((b, N_HEADS, t, QK_WIDTH), BF16)),
        rows(KV_RANK, F32),
        rows(QK_ROPE_DIM, F32),
        rows(QK_WIDTH, BF16),
        rows(POOL_WIDTH, BF16),
        rows(MEM_WIDTH, BF16),
        rows(D_MIX, BF16),
        (pl.BlockSpec((None, POOL_STATE_LEN, POOL_WIDTH), lambda i, j: (i, 0, 0)),
         jax.ShapeDtypeStruct((b, POOL_STATE_LEN, POOL_WIDTH), F32)),
    ]
    mem_spec = pl.BlockSpec((None, MEM_TOKENS, MEM_WIDTH), lambda i, j: (i, 0, 0))
    in_specs = [
        pl.BlockSpec((None, tm, D_MODEL), lambda i, j: (i, j, 0)),
        pl.BlockSpec((tm, ROPE_PAD), lambda i, j: (j, 0)),
        pl.BlockSpec((tm, ROPE_PAD), lambda i, j: (j, 0)),
    ] + [_const_spec(w.shape) for w in wts] + [mem_spec, mem_spec]
    return pl.pallas_call(
        _proj_prompt_kernel,
        grid=(b, nj),
        in_specs=in_specs,
        out_specs=[o[0] for o in outs],
        out_shape=[o[1] for o in outs],
        scratch_shapes=[pltpu.VMEM((16 + tm, POOL_WIDTH), F32)],
        compiler_params=_params(2),
        name="proj_prompt",
    )(x, cs, sn, *wts, mkb, mvb)


def _attn_prompt_kernel(q_ref, kc_ref, wvb_ref, o_ref, s_even, s_odd, m_ref, l_ref, acc_ref):
    i = pl.program_id(1)
    rows = N_HEADS * ATT_Q
    sub = ATT_SUB_HEADS * ATT_Q
    n_sub = N_HEADS // ATT_SUB_HEADS
    m_ref[...] = jnp.full((rows, 1), NEG_BIG, F32)
    l_ref[...] = jnp.zeros((rows, 1), F32)
    acc_ref[...] = jnp.zeros((rows, KV_RANK), F32)

    def keys(c):
        return kc_ref[pl.ds(pl.multiple_of(c * ATT_K, ATT_K), ATT_K), :]

    def scores(c, s_ref):
        k = keys(c)
        for sb in range(n_sub):
            q = q_ref[sb * ATT_SUB_HEADS:(sb + 1) * ATT_SUB_HEADS].reshape(sub, QK_WIDTH)
            s_ref[sb * sub:(sb + 1) * sub, :] = _dot_nt(q, k)

    def attend(c, s_ref, diag_width=None):
        masked = diag_width is not None
        width = diag_width if masked else ATT_K
        v = kc_ref[pl.ds(pl.multiple_of(c * ATT_K, ATT_K), width), 0:KV_RANK]
        if masked:
            key_pos = c * ATT_K + lax.broadcasted_iota(jnp.int32, (ATT_Q, width), 1)
            q_pos = i * ATT_Q + lax.broadcasted_iota(jnp.int32, (ATT_Q, width), 0)
            visible = (key_pos <= q_pos)[None]
        for sb in range(n_sub):
            r = slice(sb * sub, (sb + 1) * sub)
            s = s_ref[r, 0:width]
            if masked:
                s = jnp.where(visible, s.reshape(ATT_SUB_HEADS, ATT_Q, width), NEG_BIG)
                s = s.reshape(sub, width)
            m_prev = m_ref[r, :]
            m_new = jnp.maximum(m_prev, jnp.max(s, axis=-1, keepdims=True))
            alpha = jnp.exp(m_prev - m_new)
            p = jnp.exp(s - m_new)
            l_ref[r, :] = alpha * l_ref[r, :] + jnp.sum(p, axis=-1, keepdims=True)
            acc_ref[r, :] = alpha * acc_ref[r, :] + _dot(p.astype(BF16), v)
            m_ref[r, :] = m_new

    n_full = (i * ATT_Q) // ATT_K
    scores(0, s_even)

    def body(c, carry):
        @pl.when(lax.rem(c, 2) == 0)
        def _():
            scores(c + 1, s_odd)
            attend(c, s_even)

        @pl.when(lax.rem(c, 2) == 1)
        def _():
            scores(c + 1, s_even)
            attend(c, s_odd)

        return carry

    lax.fori_loop(0, n_full, body, 0)

    blocks_per_chunk = ATT_K // ATT_Q
    for parity, s_ref in enumerate((s_even, s_odd)):
        for d in range(blocks_per_chunk):
            @pl.when(jnp.logical_and(lax.rem(n_full, 2) == parity, lax.rem(i, blocks_per_chunk) == d))
            def _(s_ref=s_ref, d=d):
                attend(n_full, s_ref, diag_width=(d + 1) * ATT_Q)

    for hh in range(N_HEADS):
        r = slice(hh * ATT_Q, (hh + 1) * ATT_Q)
        o_h = (acc_ref[r, :] / l_ref[r, :]).astype(BF16)
        o_ref[:, hh * V_HEAD_DIM:(hh + 1) * V_HEAD_DIM] = _dot(o_h, wvb_ref[hh]).astype(BF16)


def _attn_prompt(q, kc, wvb):
    b, _, t, _ = q.shape
    rows = N_HEADS * ATT_Q
    return pl.pallas_call(
        _attn_prompt_kernel,
        grid=(b, t // ATT_Q),
        in_specs=[pl.BlockSpec((None, N_HEADS, ATT_Q, QK_WIDTH), lambda bi, i: (bi, 0, i, 0)),
                  pl.BlockSpec((None, t, QK_WIDTH), lambda bi, i: (bi, 0, 0)),
                  _const_spec(wvb.shape)],
        out_specs=pl.BlockSpec((None, ATT_Q, MLA_WIDTH), lambda bi, i: (bi, i, 0)),
        out_shape=jax.ShapeDtypeStruct((b, t, MLA_WIDTH), BF16),
        scratch_shapes=[pltpu.VMEM((rows, ATT_K), F32), pltpu.VMEM((rows, ATT_K), F32),
                        pltpu.VMEM((rows, 1), F32), pltpu.VMEM((rows, 1), F32),
                        pltpu.VMEM((rows, KV_RANK), F32)],
        compiler_params=_params(2),
        name="attn_prompt",
    )(q, kc, wvb)


def _combine_kernel(x_ref, mla_ref, pool_ref, cross_ref, sg_ref, wout_ref, fn_ref, y_ref):
    def part(ref, a, b):
        mix = ref[...].astype(F32) * sg_ref[:, a:b].astype(F32)
        return _dot(mix.astype(BF16), wout_ref[a:b, :])

    acc = x_ref[...] + part(mla_ref, 0, MLA_WIDTH)
    acc = acc + part(pool_ref, MLA_WIDTH, MLA_WIDTH + POOL_WIDTH)
    acc = acc + part(cross_ref, MLA_WIDTH + POOL_WIDTH, D_MIX)
    y_ref[...] = _rms(acc, fn_ref[...])


def _combine(x, mla, pool, cross, sg, wout, fn):
    n = x.shape[0]
    tm = PROJ_ROWS

    def rows(width):
        return pl.BlockSpec((tm, width), lambda i: (i, 0))

    return pl.pallas_call(
        _combine_kernel,
        grid=(n // tm,),
        in_specs=[rows(D_MODEL), rows(MLA_WIDTH), rows(POOL_WIDTH), rows(MEM_WIDTH), rows(D_MIX),
                  _const_spec(wout.shape), _const_spec(fn.shape)],
        out_specs=rows(D_MODEL),
        out_shape=jax.ShapeDtypeStruct((n, D_MODEL), F32),
        compiler_params=_params(1),
        name="combine",
    )(x, mla, pool, cross, sg, wout, fn)


def _proj_sample_kernel(x_ref, cs_ref, sn_ref, an_ref, wa_ref, wm_ref, wt_ref, qan_ref, wq_ref,
                        kvn_ref, wkb_ref, wpool_ref, pscale_ref, st_ref,
                        q_ref, ckv_ref, kr_ref, kc_ref, pool_ref, qm_ref, sg_ref, pst_ref,
                        ext_ref):
    w1_ref = (wa_ref, wm_ref, wt_ref)
    tm = PROJ_ROWS
    ns = tm // DEC_SEQ
    cs = jnp.broadcast_to(cs_ref[...][None], (ns, DEC_SEQ, ROPE_PAD)).reshape(tm, ROPE_PAD)
    sn = jnp.broadcast_to(sn_ref[...][None], (ns, DEC_SEQ, ROPE_PAD)).reshape(tm, ROPE_PAD)
    h = _normed_input(x_ref[...], an_ref)
    u_qm = _segment(h, w1_ref, C_U, C_GATE)
    u = u_qm[:, 0:POOL_WIDTH]
    qm_ref[...] = u_qm[:, POOL_WIDTH:]
    q_up = _query_up(_query_low_rank(h, w1_ref, qan_ref), wq_ref)
    for hh, (q_lat, q_r) in enumerate(_query_heads(q_up, cs, sn, wkb_ref)):
        q_ref[:, hh, :, 0:KV_RANK] = q_lat.reshape(ns, DEC_SEQ, KV_RANK)
        q_ref[:, hh, :, KV_RANK:QK_WIDTH] = q_r.reshape(ns, DEC_SEQ, ROPE_PAD)
    ckv, kr = _latent_key(h, cs, sn, w1_ref, kvn_ref)
    ckv_ref[...] = ckv
    kr_ref[...] = kr[:, :QK_ROPE_DIM]
    kc_ref[:, 0:KV_RANK] = ckv
    kc_ref[:, KV_RANK:QK_WIDTH] = kr
    for part in range(GATE_PARTS):
        _silu_gate_part(h, w1_ref, sg_ref, part)

    base = POOL_STATE_LEN + 1
    ext_ref[:, 1:base, :] = st_ref[...]
    ext_ref[:, base:base + DEC_SEQ, :] = u.reshape(ns, DEC_SEQ, POOL_WIDTH)
    for g, w in enumerate(POOL_WINDOWS):
        sl = slice(g * POOL_GROUP_DIM, (g + 1) * POOL_GROUP_DIM)
        acc = ext_ref[:, base:base + DEC_SEQ, sl]
        for k in range(1, w):
            acc = acc + ext_ref[:, base - k:base - k + DEC_SEQ, sl]
        diff = (acc * (1.0 / w)).reshape(tm, POOL_GROUP_DIM) - u[:, sl]
        y = _dot(diff.astype(BF16), wpool_ref[g]) * pscale_ref[:, sl]
        pool_ref[:, sl] = y.astype(BF16)
    pst_ref[...] = ext_ref[:, base + DEC_SEQ - POOL_STATE_LEN:base + DEC_SEQ, :]


def _proj_sample(x, cs, sn, wts, state):
    n = x.shape[0]
    tm = PROJ_ROWS
    ns = tm // DEC_SEQ
    nseq = n // DEC_SEQ

    def rows(width, dtype):
        return (pl.BlockSpec((tm, width), lambda i: (i, 0)), jax.ShapeDtypeStruct((n, width), dtype))

    outs = [
        (pl.BlockSpec((ns, N_HEADS, DEC_SEQ, QK_WIDTH), lambda i: (i, 0, 0, 0)),
         jax.ShapeDtypeStruct((nseq, N_HEADS, DEC_SEQ, QK_WIDTH), F32)),
        rows(KV_RANK, F32),
        rows(QK_ROPE_DIM, F32),
        rows(QK_WIDTH, F32),
        rows(POOL_WIDTH, BF16),
        rows(MEM_WIDTH, F32),
        rows(D_MIX, BF16),
        (pl.BlockSpec((ns, POOL_STATE_LEN, POOL_WIDTH), lambda i: (i, 0, 0)),
         jax.ShapeDtypeStruct((nseq, POOL_STATE_LEN, POOL_WIDTH), F32)),
    ]
    in_specs = [
        pl.BlockSpec((tm, D_MODEL), lambda i: (i, 0)),
        _const_spec(cs.shape),
        _const_spec(sn.shape),
    ] + [_const_spec(w.shape) for w in wts] + [
        pl.BlockSpec((ns, POOL_STATE_LEN, POOL_WIDTH), lambda i: (i, 0, 0))]
    return pl.pallas_call(
        _proj_sample_kernel,
        grid=(n // tm,),
        in_specs=in_specs,
        out_specs=[o[0] for o in outs],
        out_shape=[o[1] for o in outs],
        scratch_shapes=[pltpu.VMEM((ns, POOL_STATE_LEN + 1 + DEC_SEQ, POOL_WIDTH), F32)],
        compiler_params=_params(1),
        name="proj_sample",
    )(x, cs, sn, *wts, state)


def _cross_sample_kernel(qm_ref, k_ref, v_ref, o_ref):
    rows = MEM_HEADS * DEC_SEQ
    cols = MEM_TOKENS * MEM_HEADS
    r_head = lax.broadcasted_iota(jnp.int32, (rows, cols), 0) // DEC_SEQ
    c_head = lax.broadcasted_iota(jnp.int32, (rows, cols), 1) % MEM_HEADS
    same_head = r_head == c_head
    scores = []
    for s_i in range(CROSS_SEQS):
        q = qm_ref[s_i * DEC_SEQ:(s_i + 1) * DEC_SEQ, :]
        q_ht = jnp.concatenate([q[:, hh * MEM_HEAD_DIM:(hh + 1) * MEM_HEAD_DIM]
                                for hh in range(MEM_HEADS)], axis=0).astype(BF16)
        scores.append(_dot_nt(q_ht, k_ref[s_i].astype(BF16)))
    probs = [_softmax_rows(jnp.where(same_head, s * MEM_SCALE, NEG_BIG)).astype(BF16) for s in scores]
    outs = []
    for s_i in range(CROSS_SEQS):
        o_ht = _dot(probs[s_i], v_ref[s_i].astype(BF16))
        outs.append(jnp.concatenate([o_ht[hh * DEC_SEQ:(hh + 1) * DEC_SEQ]
                                     for hh in range(MEM_HEADS)], axis=-1))
    o_ref[...] = jnp.concatenate(outs, axis=0).astype(BF16)


def _cross_sample(qm, mem_k, mem_v):
    n = qm.shape[0]
    tm = CROSS_SEQS * DEC_SEQ
    kv_spec = pl.BlockSpec((CROSS_SEQS, MEM_TOKENS * MEM_HEADS, MEM_HEAD_DIM), lambda i: (i, 0, 0))
    return pl.pallas_call(
        _cross_sample_kernel,
        grid=(n // tm,),
        in_specs=[pl.BlockSpec((tm, MEM_WIDTH), lambda i: (i, 0)), kv_spec, kv_spec],
        out_specs=pl.BlockSpec((tm, MEM_WIDTH), lambda i: (i, 0)),
        out_shape=jax.ShapeDtypeStruct((n, MEM_WIDTH), BF16),
        compiler_params=_params(1),
        name="cross_sample",
    )(qm, mem_k, mem_v)


def _page_copies(pages, ckv_hbm, krt_hbm, ckv_buf, krt_buf, sem, slot):
    copies = []
    for j, pg in enumerate(pages):
        copies.append(pltpu.make_async_copy(
            ckv_hbm.at[pg], ckv_buf.at[slot, pl.ds(j * PAGE_SIZE, PAGE_SIZE), :], sem.at[slot]))
        copies.append(pltpu.make_async_copy(krt_hbm.at[pg], krt_buf.at[slot, j], sem.at[slot]))
    return copies


def _attn_sample_kernel(pt_ref, q_ref, qn_ref, knew_ref, ckv_hbm, krt_hbm, o_ref,
                        ckv_buf, krt_buf, sem, s_buf, kcb_buf, m_ref, l_ref, acc_ref):
    npg = PAGES_PER_STEP
    nch = N_PAGES // npg
    s_i = pl.program_id(0)
    c = pl.program_id(1)
    step = s_i * nch + c
    last_step = pl.num_programs(0) * nch - 1
    parity = lax.rem(step, 2)
    rows = N_HEADS * DEC_SEQ

    def advance(seq, chunk):
        wrap = chunk + 1 == nch
        return jnp.where(wrap, seq + 1, seq), jnp.where(wrap, 0, chunk + 1)

    def start_fetch(seq, chunk, slot):
        pages = [pt_ref[seq, chunk * npg + j] for j in range(npg)]
        for n, cp in enumerate(_page_copies(pages, ckv_hbm, krt_hbm, ckv_buf, krt_buf, sem, slot)):
            cp.start(priority=(n // 2) % 2)

    def wait(slot):
        for cp in _page_copies([0] * npg, ckv_hbm, krt_hbm, ckv_buf, krt_buf, sem, slot):
            cp.wait()

    def scores(q_rows, p):
        kc = ckv_buf[p].astype(BF16)
        krt = jnp.concatenate([krt_buf[p, j] for j in range(npg)], axis=-1).astype(BF16)
        kcb_buf[p] = kc
        s_buf[p] = (_dot_nt(q_rows[:, 0:KV_RANK], kc)
                    + _dot(q_rows[:, KV_RANK:KV_RANK + QK_ROPE_DIM], krt))

    seq1, chunk1 = advance(s_i, c)
    seq2, chunk2 = advance(seq1, chunk1)
    q_f32 = q_ref[...].reshape(rows, QK_WIDTH)
    q = q_f32.astype(BF16)

    @pl.when(step == 0)
    def _():
        start_fetch(s_i, c, 0)
        start_fetch(seq1, chunk1, 1)
        wait(0)
        scores(q, 0)

    @pl.when(step + 2 <= last_step)
    def _():
        start_fetch(seq2, chunk2, parity)

    @pl.when(step < last_step)
    def _():
        wait(1 - parity)

    @pl.when(c == 0)
    def _():
        pad = jnp.zeros((PAGE_SIZE - DEC_SEQ, QK_WIDTH), F32)
        kn = jnp.concatenate([knew_ref[...], pad], axis=0).astype(BF16)
        s = _dot_nt(q, kn)
        key_j = lax.broadcasted_iota(jnp.int32, (rows, PAGE_SIZE), 1)
        q_t = lax.broadcasted_iota(jnp.int32, (rows, PAGE_SIZE), 0) % DEC_SEQ
        s = jnp.where(key_j <= q_t, s, NEG_BIG)
        m = jnp.max(s, axis=-1, keepdims=True)
        p = jnp.exp(s - m)
        m_ref[...] = m
        l_ref[...] = jnp.sum(p, axis=-1, keepdims=True)
        acc_ref[...] = _dot(p.astype(BF16), kn[:, 0:KV_RANK])

    q_next = jnp.where(c == nch - 1, qn_ref[...].reshape(rows, QK_WIDTH), q_f32).astype(BF16)

    def attend_and_score_next(p):
        scores(q_next, 1 - p)
        s = s_buf[p]
        m_prev = m_ref[...]
        m_new = jnp.maximum(m_prev, jnp.max(s, axis=-1, keepdims=True))
        alpha = jnp.exp(m_prev - m_new)
        prob = jnp.exp(s - m_new)
        l_ref[...] = alpha * l_ref[...] + jnp.sum(prob, axis=-1, keepdims=True)
        acc_ref[...] = alpha * acc_ref[...] + _dot(prob.astype(BF16), kcb_buf[p])
        m_ref[...] = m_new

    @pl.when(parity == 0)
    def _():
        attend_and_score_next(0)

    @pl.when(parity == 1)
    def _():
        attend_and_score_next(1)

    @pl.when(c == nch - 1)
    def _():
        o_ref[...] = (acc_ref[...] / l_ref[...]).reshape(N_HEADS, DEC_SEQ, KV_RANK)


def _attn_sample(page_table, q, knew, cache_ckv, cache_krope_t):
    nseq = q.shape[0]
    npg = PAGES_PER_STEP
    rows = N_HEADS * DEC_SEQ
    keys = npg * PAGE_SIZE
    q_block = (None, N_HEADS, DEC_SEQ, QK_WIDTH)
    grid_spec = pltpu.PrefetchScalarGridSpec(
        num_scalar_prefetch=1,
        grid=(nseq, N_PAGES // npg),
        in_specs=[pl.BlockSpec(q_block, lambda s, c, pt: (s, 0, 0, 0)),
                  pl.BlockSpec(q_block, lambda s, c, pt: (jnp.minimum(s + 1, nseq - 1), 0, 0, 0)),
                  pl.BlockSpec((DEC_SEQ, QK_WIDTH), lambda s, c, pt: (s, 0)),
                  pl.BlockSpec(memory_space=pl.ANY),
                  pl.BlockSpec(memory_space=pl.ANY)],
        out_specs=pl.BlockSpec((None, N_HEADS, DEC_SEQ, KV_RANK), lambda s, c, pt: (s, 0, 0, 0)),
        scratch_shapes=[pltpu.VMEM((2, keys, KV_RANK), F32),
                        pltpu.VMEM((2, npg, QK_ROPE_DIM, PAGE_SIZE), F32),
                        pltpu.SemaphoreType.DMA((2,)),
                        pltpu.VMEM((2, rows, keys), F32),
                        pltpu.VMEM((2, keys, KV_RANK), BF16),
                        pltpu.VMEM((rows, 1), F32), pltpu.VMEM((rows, 1), F32),
                        pltpu.VMEM((rows, KV_RANK), F32)],
    )
    return pl.pallas_call(
        _attn_sample_kernel,
        grid_spec=grid_spec,
        out_shape=jax.ShapeDtypeStruct((nseq, N_HEADS, DEC_SEQ, KV_RANK), F32),
        compiler_params=_params(2),
        name="attn_sample",
    )(page_table, q, q, knew, cache_ckv, cache_krope_t)


def _vup_sample_kernel(o_ref, wvb_ref, mla_ref):
    ns = PROJ_ROWS // DEC_SEQ
    for hh in range(N_HEADS):
        o_h = o_ref[:, hh, :, :].reshape(ns * DEC_SEQ, KV_RANK).astype(BF16)
        mla_ref[:, hh * V_HEAD_DIM:(hh + 1) * V_HEAD_DIM] = _dot(o_h, wvb_ref[hh]).astype(BF16)


def _vup_sample(o_lat, wvb):
    nseq = o_lat.shape[0]
    ns = PROJ_ROWS // DEC_SEQ
    return pl.pallas_call(
        _vup_sample_kernel,
        grid=(nseq // ns,),
        in_specs=[pl.BlockSpec((ns, N_HEADS, DEC_SEQ, KV_RANK), lambda i: (i, 0, 0, 0)),
                  _const_spec(wvb.shape)],
        out_specs=pl.BlockSpec((PROJ_ROWS, MLA_WIDTH), lambda i: (i, 0)),
        out_shape=jax.ShapeDtypeStruct((nseq * DEC_SEQ, MLA_WIDTH), BF16),
        compiler_params=_params(1),
        name="vup_sample",
    )(o_lat, wvb)


def _rope_tables(pos):
    inv = ROPE_THETA ** (-jnp.arange(0, QK_ROPE_DIM, 2, dtype=F32) / QK_ROPE_DIM)
    ang = pos.astype(F32)[:, None] * inv[None, :]
    cos, sin = jnp.cos(ang), jnp.sin(ang)
    reps = ROPE_PAD // QK_ROPE_DIM
    cs = jnp.tile(jnp.concatenate([cos, cos], axis=-1), (1, reps))
    sn = jnp.tile(jnp.concatenate([-sin, sin], axis=-1), (1, reps))
    return cs, sn


def _swap_halves(w):
    half = w.shape[-1] // 2
    return jnp.concatenate([w[..., half:], w[..., :half]], axis=-1)


def _pad_lanes(w, width):
    return jnp.pad(w, [(0, 0)] * (w.ndim - 1) + [(0, width - w.shape[-1])])


def _split_w_in_kernel(w_ref, wa_ref, wm_ref, wt_ref):
    s0 = Q_RANK
    s1 = s0 + KV_RANK
    s2 = s1 + QK_ROPE_DIM
    half = QK_ROPE_DIM // 2
    w = w_ref[...]
    rows = w.shape[0]
    zeros = jnp.zeros((rows, ROPE_PAD - QK_ROPE_DIM), F32)
    wa_ref[...] = w[:, 0:s0].astype(BF16)
    wm_ref[:, 0:KV_RANK] = w[:, s0:s1].astype(BF16)
    wm_ref[:, KV_RANK:KV_RANK + ROPE_PAD] = jnp.concatenate([w[:, s1:s2], zeros], axis=1).astype(BF16)
    wm_ref[:, KV_RANK + ROPE_PAD:] = jnp.concatenate(
        [w[:, s1 + half:s2], w[:, s1:s1 + half], zeros], axis=1).astype(BF16)
    wt_ref[...] = w[:, s2:].astype(BF16)


def _split_w_in(w_in):
    d, n = w_in.shape
    tm = PROJ_ROWS
    widths = (Q_RANK, KV_RANK + 2 * ROPE_PAD, n - (Q_RANK + KV_RANK + QK_ROPE_DIM))
    return pl.pallas_call(
        _split_w_in_kernel,
        grid=(d // tm,),
        in_specs=[pl.BlockSpec((tm, n), lambda i: (i, 0))],
        out_specs=[pl.BlockSpec((tm, wd), lambda i: (i, 0)) for wd in widths],
        out_shape=[jax.ShapeDtypeStruct((d, wd), BF16) for wd in widths],
        compiler_params=_params(1),
        name="split_w_in",
    )(w_in)


def _layout_weights(attn_norm, w_in, q_a_norm, w_q_b, kv_a_norm, w_k_b, w_pool, pool_scale):
    w_a, w_m, w_t = _split_w_in(w_in)
    wq_nope = w_q_b[:, :, :QK_NOPE_DIM].reshape(Q_RANK, Q_NOPE_W)
    wq_r = w_q_b[:, :, QK_NOPE_DIM:]
    wq_rope = _pad_lanes(wq_r, ROPE_PAD).reshape(Q_RANK, Q_ROPE_W)
    wq_swap = _pad_lanes(_swap_halves(wq_r), ROPE_PAD).reshape(Q_RANK, Q_ROPE_W)
    wq = jnp.concatenate([wq_nope, wq_rope, wq_swap], axis=1).astype(BF16)
    wkb = jnp.transpose(w_k_b, (1, 2, 0)).astype(BF16)
    zero = jnp.zeros_like(wkb[0::2])
    wkb = jnp.concatenate([jnp.concatenate([wkb[0::2], zero], axis=2),
                           jnp.concatenate([zero, wkb[1::2]], axis=2)], axis=1)
    return (attn_norm.reshape(1, D_MODEL), w_a, w_m, w_t, q_a_norm.reshape(1, Q_RANK), wq,
            kv_a_norm.reshape(1, KV_RANK), wkb, w_pool.astype(BF16),
            pool_scale.reshape(1, POOL_WIDTH))


def kernel(x_prompt, x_sample, mem_prompt, cache_ckv, cache_krope, page_table, state_pool,
           cache_mem_k, cache_mem_v, attn_norm, mem_norm, w_in, q_a_norm, w_q_b, kv_a_norm,
           w_k_b, w_v_b, w_pool, pool_scale, w_mem_kv, w_out, final_norm):
    assert DEPTH == 1 and attn_norm.shape[0] == 1
    n_phys = cache_ckv.shape[1]
    wts = _layout_weights(attn_norm[0], w_in[0], q_a_norm[0], w_q_b[0], kv_a_norm[0], w_k_b[0],
                          w_pool[0], pool_scale[0])
    wvb = jnp.transpose(w_v_b[0], (1, 0, 2)).astype(BF16)
    wout = w_out[0].astype(BF16)
    fn = final_norm.reshape(1, D_MODEL)

    mk, mv, mkb, mvb = _mem_kv(mem_prompt, mem_norm[0].reshape(1, D_MODEL), w_mem_kv[0].astype(BF16))
    cs_p, sn_p = _rope_tables(jnp.arange(SEQ))
    q_p, ckv_p, kr_p, kc_p, pool_p, cross_p, sg_p, pst_p = _proj_prompt(
        x_prompt, cs_p, sn_p, wts, mkb, mvb)
    mla_p = _attn_prompt(q_p, kc_p, wvb)
    rows_p = BATCH * SEQ
    y_p = _combine(x_prompt.reshape(rows_p, D_MODEL), mla_p.reshape(rows_p, MLA_WIDTH),
                   pool_p.reshape(rows_p, POOL_WIDTH), cross_p.reshape(rows_p, MEM_WIDTH),
                   sg_p.reshape(rows_p, D_MIX), wout, fn)

    rows_s = DEC_BATCH * DEC_SEQ
    cs_s, sn_s = _rope_tables(PAST_LEN + jnp.arange(DEC_SEQ))
    xs = x_sample.reshape(rows_s, D_MODEL)
    q_s, ckv_s, kr_s, kc_s, pool_s, qm_s, sg_s, pst_s = _proj_sample(
        xs, cs_s, sn_s, wts, state_pool[0])
    cross_s = _cross_sample(
        qm_s,
        cache_mem_k.reshape(DEC_BATCH, MEM_TOKENS * MEM_HEADS, MEM_HEAD_DIM),
        cache_mem_v.reshape(DEC_BATCH, MEM_TOKENS * MEM_HEADS, MEM_HEAD_DIM))
    o_lat_s = _attn_sample(page_table, q_s, kc_s,
                           cache_ckv.reshape(n_phys, PAGE_SIZE, KV_RANK),
                           jnp.swapaxes(cache_krope.reshape(n_phys, PAGE_SIZE, QK_ROPE_DIM), 1, 2))
    mla_s = _vup_sample(o_lat_s, wvb)
    y_s = _combine(xs, mla_s, pool_s, cross_s, sg_s, wout, fn)

    return (y_p.reshape(BATCH, SEQ, D_MODEL),
            y_s.reshape(DEC_BATCH, DEC_SEQ, D_MODEL),
            ckv_p[None],
            kr_p[None],
            pst_p[None],
            mk.reshape(1, BATCH, MEM_TOKENS, MEM_HEADS, MEM_HEAD_DIM),
            mv.reshape(1, BATCH, MEM_TOKENS, MEM_HEADS, MEM_HEAD_DIM),
            ckv_s.reshape(1, DEC_BATCH, DEC_SEQ, KV_RANK),
            kr_s.reshape(1, DEC_BATCH, DEC_SEQ, QK_ROPE_DIM),
            pst_s[None])
```

```python
import functools

import jax
import jax.numpy as jnp
from jax import lax
from jax.experimental import pallas as pl
from jax.experimental.pallas import tpu as pltpu

F32 = jnp.float32
BF16 = jnp.bfloat16

D_MODEL = 2048
BATCH = 2
SEQ = 4096
DEPTH = 1
DEC_BATCH = 128
DEC_SEQ = 8
PAST_LEN = 16384
PAGE_SIZE = 128
N_PAGES = PAST_LEN // PAGE_SIZE

N_HEADS = 8
QK_NOPE_DIM = 128
QK_ROPE_DIM = 64
V_HEAD_DIM = 128
Q_RANK = 512
KV_RANK = 256
MLA_WIDTH = N_HEADS * V_HEAD_DIM
MLA_SCALE = (QK_NOPE_DIM + QK_ROPE_DIM) ** -0.5
ROPE_THETA = 10000.0
POOL_GROUPS = 4
POOL_WINDOWS = (2, 4, 8, 16)
POOL_GROUP_DIM = 128
POOL_WIDTH = POOL_GROUPS * POOL_GROUP_DIM
POOL_STATE_LEN = 15
MEM_TOKENS = 256
MEM_HEADS = 4
MEM_HEAD_DIM = 128
MEM_WIDTH = MEM_HEADS * MEM_HEAD_DIM
MEM_SCALE = MEM_HEAD_DIM ** -0.5
D_MIX = MLA_WIDTH + POOL_WIDTH + MEM_WIDTH
RMS_EPS = 1e-6

LANE = 128
ROPE_PAD = LANE
QK_WIDTH = KV_RANK + ROPE_PAD
C_QA = 0
C_CKV = C_QA + Q_RANK
C_KR = C_CKV + KV_RANK
C_U = C_KR + 2 * ROPE_PAD
C_QM = C_U + POOL_WIDTH
C_GATE = C_QM + MEM_WIDTH
C_END = C_GATE + D_MIX
Q_NOPE_W = N_HEADS * QK_NOPE_DIM
Q_ROPE_W = N_HEADS * ROPE_PAD

VMEM_LIMIT = 56 * 1024 * 1024
NEG_BIG = -1e30

PROJ_ROWS = 256
GATE_PARTS = 4
ATT_Q = 256
ATT_K = 512
ATT_SUB_HEADS = 1
PAGES_PER_STEP = 32
CROSS_SEQS = 8


def _const_spec(shape):
    n = len(shape)
    return pl.BlockSpec(shape, lambda *_: (0,) * n, pipeline_mode=pl.Buffered(1))


def _params(n_axes):
    return pltpu.CompilerParams(dimension_semantics=("arbitrary",) * n_axes,
                                vmem_limit_bytes=VMEM_LIMIT)


def _rms(x, g):
    return x * lax.rsqrt(jnp.mean(x * x, axis=-1, keepdims=True) + RMS_EPS) * g


def _dot(a, b):
    return jnp.dot(a, b, preferred_element_type=F32)


def _dot_nt(a, b):
    return lax.dot_general(a, b, (((1,), (1,)), ((), ())), preferred_element_type=F32)


def _softmax_rows(s):
    m = jnp.max(s, axis=-1, keepdims=True)
    e = jnp.exp(s - m)
    return e / jnp.sum(e, axis=-1, keepdims=True)


def _mem_kv_kernel(mem_ref, g_ref, w_ref, k_ref, v_ref, kb_ref, vb_ref):
    h = _rms(mem_ref[...], g_ref[...]).astype(BF16)
    kv = _dot(h, w_ref[...])
    k = kv[:, :MEM_WIDTH]
    v = kv[:, MEM_WIDTH:]
    k_ref[...] = k
    v_ref[...] = v
    kb_ref[...] = k.astype(BF16)
    vb_ref[...] = v.astype(BF16)


def _mem_kv(mem, mem_norm, w_mem_kv):
    b = mem.shape[0]
    out_f32 = jax.ShapeDtypeStruct((b, MEM_TOKENS, MEM_WIDTH), F32)
    out_bf16 = jax.ShapeDtypeStruct((b, MEM_TOKENS, MEM_WIDTH), BF16)
    blk = pl.BlockSpec((None, MEM_TOKENS, MEM_WIDTH), lambda i: (i, 0, 0))
    return pl.pallas_call(
        _mem_kv_kernel,
        grid=(b,),
        in_specs=[pl.BlockSpec((None, MEM_TOKENS, D_MODEL), lambda i: (i, 0, 0)),
                  _const_spec((1, D_MODEL)),
                  _const_spec((D_MODEL, 2 * MEM_WIDTH))],
        out_specs=[blk, blk, blk, blk],
        out_shape=[out_f32, out_f32, out_bf16, out_bf16],
        compiler_params=_params(1),
        name="mem_kv",
    )(mem, mem_norm, w_mem_kv)


def _normed_input(x, an_ref):
    return _rms(x, an_ref[...]).astype(BF16)


def _segment(h, w1_ref, a, b):
    wt_ref, wrope_ref = w1_ref
    rope_end = Q_RANK + KV_RANK + QK_ROPE_DIM
    if b <= C_KR:
        return _dot_nt(h, wt_ref[a:b, :])
    if C_KR <= a and b <= C_U:
        return _dot_nt(h, wrope_ref[a - C_KR:b - C_KR, :])
    if C_U <= a:
        return _dot_nt(h, wt_ref[a - C_U + rope_end:b - C_U + rope_end, :])
    raise ValueError(f"columns [{a}, {b}) straddle two weight groups")


def _query_low_rank(h, w1_ref, qan_ref):
    return _rms(_segment(h, w1_ref, C_QA, C_CKV), qan_ref[...]).astype(BF16)


def _query_up(qa, wq_ref):
    q_nope = _dot(qa, wq_ref[:, 0:Q_NOPE_W])
    q_rope = _dot(qa, wq_ref[:, Q_NOPE_W:Q_NOPE_W + Q_ROPE_W])
    q_swap = _dot(qa, wq_ref[:, Q_NOPE_W + Q_ROPE_W:Q_NOPE_W + 2 * Q_ROPE_W])
    return q_nope, q_rope, q_swap


def _query_heads(q_up, cs, sn, wkb_ref):
    q_nope, q_rope, q_swap = q_up
    heads = []
    for pair in range(N_HEADS // 2):
        both = slice(2 * pair * QK_NOPE_DIM, 2 * (pair + 1) * QK_NOPE_DIM)
        q_lat2 = _dot(q_nope[:, both].astype(BF16), wkb_ref[pair]) * MLA_SCALE
        for k in range(2):
            sl = slice((2 * pair + k) * LANE, (2 * pair + k + 1) * LANE)
            q_r = (q_rope[:, sl] * cs + q_swap[:, sl] * sn) * MLA_SCALE
            heads.append((q_lat2[:, k * KV_RANK:(k + 1) * KV_RANK], q_r))
    return heads


def _latent_key(h, cs, sn, w1_ref, kvn_ref):
    ckv = _rms(_segment(h, w1_ref, C_CKV, C_KR), kvn_ref[...])
    zk = _segment(h, w1_ref, C_KR, C_U)
    kr = zk[:, :ROPE_PAD] * cs + zk[:, ROPE_PAD:] * sn
    return ckv, kr


def _silu_gate_part(h, w1_ref, sg_ref, part):
    width = D_MIX // GATE_PARTS
    g = _segment(h, w1_ref, C_GATE + part * width, C_GATE + (part + 1) * width)
    sg_ref[:, part * width:(part + 1) * width] = (g / (1.0 + jnp.exp(-g))).astype(BF16)


def _proj_prompt_kernel(x_ref, cs_ref, sn_ref, an_ref, wt_ref, wrope_ref, qan_ref, wq_ref,
                        kvn_ref, wkb_ref, wpool_ref, pscale_ref, mk_ref, mv_ref,
                        q_ref, ckv_ref, kr_ref, kc_ref, pool_ref, cross_ref, sg_ref, pst_ref,
                        uext_ref):
    w1_ref = (wt_ref, wrope_ref)
    j = pl.program_id(1)
    tm = PROJ_ROWS
    halo = 16

    @pl.when(j == 0)
    def _():
        uext_ref[0:halo, :] = jnp.zeros((halo, POOL_WIDTH), F32)

    cs = cs_ref[...]
    sn = sn_ref[...]
    h = _normed_input(x_ref[...], an_ref)
    u_qm = _segment(h, w1_ref, C_U, C_GATE)
    u = u_qm[:, 0:POOL_WIDTH]
    qm = u_qm[:, POOL_WIDTH:]
    qa = _query_low_rank(h, w1_ref, qan_ref)
    _silu_gate_part(h, w1_ref, sg_ref, 0)

    head_lanes = [slice(hh * MEM_HEAD_DIM, (hh + 1) * MEM_HEAD_DIM) for hh in range(MEM_HEADS)]
    scores = [_dot_nt(qm[:, sl].astype(BF16), mk_ref[:, sl]) for sl in head_lanes]
    probs = [_softmax_rows(s * MEM_SCALE).astype(BF16) for s in scores]
    for sl, p in zip(head_lanes, probs):
        cross_ref[:, sl] = _dot(p, mv_ref[:, sl]).astype(BF16)
    q_up = _query_up(qa, wq_ref)
    _silu_gate_part(h, w1_ref, sg_ref, 1)

    uext_ref[halo:halo + tm, :] = u
    pos = j * tm + lax.broadcasted_iota(jnp.int32, (tm, 1), 0)
    group_lanes = [slice(g * POOL_GROUP_DIM, (g + 1) * POOL_GROUP_DIM) for g in range(POOL_GROUPS)]
    diffs = []
    for sl, w in zip(group_lanes, POOL_WINDOWS):
        acc = u[:, sl]
        for k in range(1, w):
            acc = acc + uext_ref[halo - k:halo - k + tm, sl]
        cnt = jnp.minimum(pos + 1, w).astype(F32)
        diffs.append((acc / cnt - u[:, sl]).astype(BF16))
    for g, (sl, diff) in enumerate(zip(group_lanes, diffs)):
        pool_ref[:, sl] = (_dot(diff, wpool_ref[g]) * pscale_ref[:, sl]).astype(BF16)
    uext_ref[0:halo, :] = uext_ref[tm:tm + halo, :]
    _silu_gate_part(h, w1_ref, sg_ref, 2)

    for hh, (q_lat, q_r) in enumerate(_query_heads(q_up, cs, sn, wkb_ref)):
        q_ref[hh, :, 0:KV_RANK] = q_lat.astype(BF16)
        q_ref[hh, :, KV_RANK:QK_WIDTH] = q_r.astype(BF16)
    _silu_gate_part(h, w1_ref, sg_ref, 3)
    ckv, kr = _latent_key(h, cs, sn, w1_ref, kvn_ref)
    ckv_ref[...] = ckv
    kr_ref[...] = kr[:, :QK_ROPE_DIM]
    kc_ref[:, 0:KV_RANK] = ckv.astype(BF16)
    kc_ref[:, KV_RANK:QK_WIDTH] = kr.astype(BF16)

    @pl.when(j == pl.num_programs(1) - 1)
    def _():
        pst_ref[...] = uext_ref[halo + tm - POOL_STATE_LEN:halo + tm, :]


def _proj_prompt(x, cs, sn, wts, mkb, mvb):
    b, t, _ = x.shape
    tm = PROJ_ROWS
    nj = t // tm

    def rows(width, dtype):
        return (pl.BlockSpec((None, tm, width), lambda i, j: (i, j, 0)),
                jax.ShapeDtypeStruct((b, t, width), dtype))

    outs = [
        (pl.BlockSpec((None, N_HEADS, tm, QK_WIDTH), lambda i, j: (i, 0, j, 0)),
         jax.ShapeDtypeStruct((b, N_HEADS, t, QK_WIDTH), BF16)),
        rows(KV_RANK, F32),
        rows(QK_ROPE_DIM, F32),
        rows(QK_WIDTH, BF16),
        rows(POOL_WIDTH, BF16),
        rows(MEM_WIDTH, BF16),
        rows(D_MIX, BF16),
        (pl.BlockSpec((None, POOL_STATE_LEN, POOL_WIDTH), lambda i, j: (i, 0, 0)),
         jax.ShapeDtypeStruct((b, POOL_STATE_LEN, POOL_WIDTH), F32)),
    ]
    mem_spec = pl.BlockSpec((None, MEM_TOKENS, MEM_WIDTH), lambda i, j: (i, 0, 0))
    in_specs = [
        pl.BlockSpec((None, tm, D_MODEL), lambda i, j: (i, j, 0)),
        pl.BlockSpec((tm, ROPE_PAD), lambda i, j: (j, 0)),
        pl.BlockSpec((tm, ROPE_PAD), lambda i, j: (j, 0)),
    ] + [_const_spec(w.shape) for w in wts] + [mem_spec, mem_spec]
    return pl.pallas_call(
        _proj_prompt_kernel,
        grid=(b, nj),
        in_specs=in_specs,
        out_specs=[o[0] for o in outs],
        out_shape=[o[1] for o in outs],
        scratch_shapes=[pltpu.VMEM((16 + tm, POOL_WIDTH), F32)],
        compiler_params=_params(2),
        name="proj_prompt",
    )(x, cs, sn, *wts, mkb, mvb)


def _attn_prompt_kernel(q_ref, kc_ref, wvb_ref, o_ref, s_even, s_odd, m_ref, l_ref, acc_ref):
    i = pl.program_id(1)
    rows = N_HEADS * ATT_Q
    sub = ATT_SUB_HEADS * ATT_Q
    n_sub = N_HEADS // ATT_SUB_HEADS
    m_ref[...] = jnp.full((rows, 1), NEG_BIG, F32)
    l_ref[...] = jnp.zeros((rows, 1), F32)
    acc_ref[...] = jnp.zeros((rows, KV_RANK), F32)

    def keys(c):
        return kc_ref[pl.ds(pl.multiple_of(c * ATT_K, ATT_K), ATT_K), :]

    def scores(c, s_ref):
        k = keys(c)
        for sb in range(n_sub):
            q = q_ref[sb * ATT_SUB_HEADS:(sb + 1) * ATT_SUB_HEADS].reshape(sub, QK_WIDTH)
            s_ref[sb * sub:(sb + 1) * sub, :] = _dot_nt(q, k)

    def attend(c, s_ref, diag_width=None):
        masked = diag_width is not None
        width = diag_width if masked else ATT_K
        v = kc_ref[pl.ds(pl.multiple_of(c * ATT_K, ATT_K), width), 0:KV_RANK]
        if masked:
            key_pos = c * ATT_K + lax.broadcasted_iota(jnp.int32, (ATT_Q, width), 1)
            q_pos = i * ATT_Q + lax.broadcasted_iota(jnp.int32, (ATT_Q, width), 0)
            visible = (key_pos <= q_pos)[None]
        for sb in range(n_sub):
            r = slice(sb * sub, (sb + 1) * sub)
            s = s_ref[r, 0:width]
            if masked:
                s = jnp.where(visible, s.reshape(ATT_SUB_HEADS, ATT_Q, width), NEG_BIG)
                s = s.reshape(sub, width)
            m_prev = m_ref[r, :]
            m_new = jnp.maximum(m_prev, jnp.max(s, axis=-1, keepdims=True))
            alpha = jnp.exp(m_prev - m_new)
            p = jnp.exp(s - m_new)
            l_ref[r, :] = alpha * l_ref[r, :] + jnp.sum(p, axis=-1, keepdims=True)
            acc_ref[r, :] = alpha * acc_ref[r, :] + _dot(p.astype(BF16), v)
            m_ref[r, :] = m_new

    n_full = (i * ATT_Q) // ATT_K
    scores(0, s_even)

    def body(c, carry):
        @pl.when(lax.rem(c, 2) == 0)
        def _():
            scores(c + 1, s_odd)
            attend(c, s_even)

        @pl.when(lax.rem(c, 2) == 1)
        def _():
            scores(c + 1, s_even)
            attend(c, s_odd)

        return carry

    lax.fori_loop(0, n_full, body, 0)

    blocks_per_chunk = ATT_K // ATT_Q
    for parity, s_ref in enumerate((s_even, s_odd)):
        for d in range(blocks_per_chunk):
            @pl.when(jnp.logical_and(lax.rem(n_full, 2) == parity, lax.rem(i, blocks_per_chunk) == d))
            def _(s_ref=s_ref, d=d):
                attend(n_full, s_ref, diag_width=(d + 1) * ATT_Q)

    for hh in range(N_HEADS):
        r = slice(hh * ATT_Q, (hh + 1) * ATT_Q)
        o_h = (acc_ref[r, :] / l_ref[r, :]).astype(BF16)
        o_ref[:, hh * V_HEAD_DIM:(hh + 1) * V_HEAD_DIM] = _dot(o_h, wvb_ref[hh]).astype(BF16)


def _attn_prompt(q, kc, wvb):
    b, _, t, _ = q.shape
    rows = N_HEADS * ATT_Q
    return pl.pallas_call(
        _attn_prompt_kernel,
        grid=(b, t // ATT_Q),
        in_specs=[pl.BlockSpec((None, N_HEADS, ATT_Q, QK_WIDTH), lambda bi, i: (bi, 0, i, 0)),
                  pl.BlockSpec((None, t, QK_WIDTH), lambda bi, i: (bi, 0, 0)),
                  _const_spec(wvb.shape)],
        out_specs=pl.BlockSpec((None, ATT_Q, MLA_WIDTH), lambda bi, i: (bi, i, 0)),
        out_shape=jax.ShapeDtypeStruct((b, t, MLA_WIDTH), BF16),
        scratch_shapes=[pltpu.VMEM((rows, ATT_K), F32), pltpu.VMEM((rows, ATT_K), F32),
                        pltpu.VMEM((rows, 1), F32), pltpu.VMEM((rows, 1), F32),
                        pltpu.VMEM((rows, KV_RANK), F32)],
        compiler_params=_params(2),
        name="attn_prompt",
    )(q, kc, wvb)


def _combine_kernel(x_ref, mla_ref, pool_ref, cross_ref, sg_ref, wout_ref, fn_ref, y_ref):
    def part(ref, a, b):
        mix = ref[...].astype(F32) * sg_ref[:, a:b].astype(F32)
        return _dot(mix.astype(BF16), wout_ref[a:b, :])

    acc = x_ref[...] + part(mla_ref, 0, MLA_WIDTH)
    acc = acc + part(pool_ref, MLA_WIDTH, MLA_WIDTH + POOL_WIDTH)
    acc = acc + part(cross_ref, MLA_WIDTH + POOL_WIDTH, D_MIX)
    y_ref[...] = _rms(acc, fn_ref[...])


def _combine(x, mla, pool, cross, sg, wout, fn):
    n = x.shape[0]
    tm = PROJ_ROWS

    def rows(width):
        return pl.BlockSpec((tm, width), lambda i: (i, 0))

    return pl.pallas_call(
        _combine_kernel,
        grid=(n // tm,),
        in_specs=[rows(D_MODEL), rows(MLA_WIDTH), rows(POOL_WIDTH), rows(MEM_WIDTH), rows(D_MIX),
                  _const_spec(wout.shape), _const_spec(fn.shape)],
        out_specs=rows(D_MODEL),
        out_shape=jax.ShapeDtypeStruct((n, D_MODEL), F32),
        compiler_params=_params(1),
        name="combine",
    )(x, mla, pool, cross, sg, wout, fn)


def _proj_sample_kernel(x_ref, cs_ref, sn_ref, an_ref, wt_ref, wrope_ref, qan_ref, wq_ref,
                        kvn_ref, wkb_ref, wpool_ref, pscale_ref, st_ref,
                        q_ref, ckv_ref, kr_ref, kc_ref, pool_ref, qm_ref, sg_ref, pst_ref,
                        ext_ref):
    w1_ref = (wt_ref, wrope_ref)
    tm = PROJ_ROWS
    ns = tm // DEC_SEQ
    cs = jnp.broadcast_to(cs_ref[...][None], (ns, DEC_SEQ, ROPE_PAD)).reshape(tm, ROPE_PAD)
    sn = jnp.broadcast_to(sn_ref[...][None], (ns, DEC_SEQ, ROPE_PAD)).reshape(tm, ROPE_PAD)
    h = _normed_input(x_ref[...], an_ref)
    u_qm = _segment(h, w1_ref, C_U, C_GATE)
    u = u_qm[:, 0:POOL_WIDTH]
    qm_ref[...] = u_qm[:, POOL_WIDTH:]
    q_up = _query_up(_query_low_rank(h, w1_ref, qan_ref), wq_ref)
    for hh, (q_lat, q_r) in enumerate(_query_heads(q_up, cs, sn, wkb_ref)):
        q_ref[:, hh, :, 0:KV_RANK] = q_lat.reshape(ns, DEC_SEQ, KV_RANK)
        q_ref[:, hh, :, KV_RANK:QK_WIDTH] = q_r.reshape(ns, DEC_SEQ, ROPE_PAD)
    ckv, kr = _latent_key(h, cs, sn, w1_ref, kvn_ref)
    ckv_ref[...] = ckv
    kr_ref[...] = kr[:, :QK_ROPE_DIM]
    kc_ref[:, 0:KV_RANK] = ckv
    kc_ref[:, KV_RANK:QK_WIDTH] = kr
    for part in range(GATE_PARTS):
        _silu_gate_part(h, w1_ref, sg_ref, part)

    base = POOL_STATE_LEN + 1
    ext_ref[:, 1:base, :] = st_ref[...]
    ext_ref[:, base:base + DEC_SEQ, :] = u.reshape(ns, DEC_SEQ, POOL_WIDTH)
    for g, w in enumerate(POOL_WINDOWS):
        sl = slice(g * POOL_GROUP_DIM, (g + 1) * POOL_GROUP_DIM)
        acc = ext_ref[:, base:base + DEC_SEQ, sl]
        for k in range(1, w):
            acc = acc + ext_ref[:, base - k:base - k + DEC_SEQ, sl]
        diff = (acc * (1.0 / w)).reshape(tm, POOL_GROUP_DIM) - u[:, sl]
        y = _dot(diff.astype(BF16), wpool_ref[g]) * pscale_ref[:, sl]
        pool_ref[:, sl] = y.astype(BF16)
    pst_ref[...] = ext_ref[:, base + DEC_SEQ - POOL_STATE_LEN:base + DEC_SEQ, :]


def _proj_sample(x, cs, sn, wts, state):
    n = x.shape[0]
    tm = PROJ_ROWS
    ns = tm // DEC_SEQ
    nseq = n // DEC_SEQ

    def rows(width, dtype):
        return (pl.BlockSpec((tm, width), lambda i: (i, 0)), jax.ShapeDtypeStruct((n, width), dtype))

    outs = [
        (pl.BlockSpec((ns, N_HEADS, DEC_SEQ, QK_WIDTH), lambda i: (i, 0, 0, 0)),
         jax.ShapeDtypeStruct((nseq, N_HEADS, DEC_SEQ, QK_WIDTH), F32)),
        rows(KV_RANK, F32),
        rows(QK_ROPE_DIM, F32),
        rows(QK_WIDTH, F32),
        rows(POOL_WIDTH, BF16),
        rows(MEM_WIDTH, F32),
        rows(D_MIX, BF16),
        (pl.BlockSpec((ns, POOL_STATE_LEN, POOL_WIDTH), lambda i: (i, 0, 0)),
         jax.ShapeDtypeStruct((nseq, POOL_STATE_LEN, POOL_WIDTH), F32)),
    ]
    in_specs = [
        pl.BlockSpec((tm, D_MODEL), lambda i: (i, 0)),
        _const_spec(cs.shape),
        _const_spec(sn.shape),
    ] + [_const_spec(w.shape) for w in wts] + [
        pl.BlockSpec((ns, POOL_STATE_LEN, POOL_WIDTH), lambda i: (i, 0, 0))]
    return pl.pallas_call(
        _proj_sample_kernel,
        grid=(n // tm,),
        in_specs=in_specs,
        out_specs=[o[0] for o in outs],
        out_shape=[o[1] for o in outs],
        scratch_shapes=[pltpu.VMEM((ns, POOL_STATE_LEN + 1 + DEC_SEQ, POOL_WIDTH), F32)],
        compiler_params=_params(1),
        name="proj_sample",
    )(x, cs, sn, *wts, state)


def _cross_sample_kernel(qm_ref, k_ref, v_ref, o_ref):
    rows = MEM_HEADS * DEC_SEQ
    cols = MEM_TOKENS * MEM_HEADS
    r_head = lax.broadcasted_iota(jnp.int32, (rows, cols), 0) // DEC_SEQ
    c_head = lax.broadcasted_iota(jnp.int32, (rows, cols), 1) % MEM_HEADS
    same_head = r_head == c_head
    scores = []
    for s_i in range(CROSS_SEQS):
        q = qm_ref[s_i * DEC_SEQ:(s_i + 1) * DEC_SEQ, :]
        q_ht = jnp.concatenate([q[:, hh * MEM_HEAD_DIM:(hh + 1) * MEM_HEAD_DIM]
                                for hh in range(MEM_HEADS)], axis=0).astype(BF16)
        scores.append(_dot_nt(q_ht, k_ref[s_i].astype(BF16)))
    probs = [_softmax_rows(jnp.where(same_head, s * MEM_SCALE, NEG_BIG)).astype(BF16) for s in scores]
    outs = []
    for s_i in range(CROSS_SEQS):
        o_ht = _dot(probs[s_i], v_ref[s_i].astype(BF16))
        outs.append(jnp.concatenate([o_ht[hh * DEC_SEQ:(hh + 1) * DEC_SEQ]
                                     for hh in range(MEM_HEADS)], axis=-1))
    o_ref[...] = jnp.concatenate(outs, axis=0).astype(BF16)


def _cross_sample(qm, mem_k, mem_v):
    n = qm.shape[0]
    tm = CROSS_SEQS * DEC_SEQ
    kv_spec = pl.BlockSpec((CROSS_SEQS, MEM_TOKENS * MEM_HEADS, MEM_HEAD_DIM), lambda i: (i, 0, 0))
    return pl.pallas_call(
        _cross_sample_kernel,
        grid=(n // tm,),
        in_specs=[pl.BlockSpec((tm, MEM_WIDTH), lambda i: (i, 0)), kv_spec, kv_spec],
        out_specs=pl.BlockSpec((tm, MEM_WIDTH), lambda i: (i, 0)),
        out_shape=jax.ShapeDtypeStruct((n, MEM_WIDTH), BF16),
        compiler_params=_params(1),
        name="cross_sample",
    )(qm, mem_k, mem_v)


def _page_copies(pages, ckv_hbm, krt_hbm, ckv_buf, krt_buf, sem, slot):
    copies = []
    for j, pg in enumerate(pages):
        copies.append(pltpu.make_async_copy(
            ckv_hbm.at[pg], ckv_buf.at[slot, pl.ds(j * PAGE_SIZE, PAGE_SIZE), :], sem.at[slot]))
        copies.append(pltpu.make_async_copy(krt_hbm.at[pg], krt_buf.at[slot, j], sem.at[slot]))
    return copies


def _attn_sample_kernel(pt_ref, q_ref, qn_ref, knew_ref, ckv_hbm, krt_hbm, o_ref,
                        ckv_buf, krt_buf, sem, s_buf, kcb_buf, m_ref, l_ref, acc_ref):
    npg = PAGES_PER_STEP
    nch = N_PAGES // npg
    s_i = pl.program_id(0)
    c = pl.program_id(1)
    step = s_i * nch + c
    last_step = pl.num_programs(0) * nch - 1
    parity = lax.rem(step, 2)
    rows = N_HEADS * DEC_SEQ

    def advance(seq, chunk):
        wrap = chunk + 1 == nch
        return jnp.where(wrap, seq + 1, seq), jnp.where(wrap, 0, chunk + 1)

    def start_fetch(seq, chunk, slot):
        pages = [pt_ref[seq, chunk * npg + j] for j in range(npg)]
        for n, cp in enumerate(_page_copies(pages, ckv_hbm, krt_hbm, ckv_buf, krt_buf, sem, slot)):
            cp.start(priority=(n // 2) % 2)

    def wait(slot):
        for cp in _page_copies([0] * npg, ckv_hbm, krt_hbm, ckv_buf, krt_buf, sem, slot):
            cp.wait()

    def scores(q_rows, p):
        kc = ckv_buf[p].astype(BF16)
        krt = jnp.concatenate([krt_buf[p, j] for j in range(npg)], axis=-1).astype(BF16)
        kcb_buf[p] = kc
        s_buf[p] = (_dot_nt(q_rows[:, 0:KV_RANK], kc)
                    + _dot(q_rows[:, KV_RANK:KV_RANK + QK_ROPE_DIM], krt))

    seq1, chunk1 = advance(s_i, c)
    seq2, chunk2 = advance(seq1, chunk1)
    q_f32 = q_ref[...].reshape(rows, QK_WIDTH)
    q = q_f32.astype(BF16)

    @pl.when(step == 0)
    def _():
        start_fetch(s_i, c, 0)
        start_fetch(seq1, chunk1, 1)
        wait(0)
        scores(q, 0)

    @pl.when(step + 2 <= last_step)
    def _():
        start_fetch(seq2, chunk2, parity)

    @pl.when(step < last_step)
    def _():
        wait(1 - parity)

    @pl.when(c == 0)
    def _():
        pad = jnp.zeros((PAGE_SIZE - DEC_SEQ, QK_WIDTH), F32)
        kn = jnp.concatenate([knew_ref[...], pad], axis=0).astype(BF16)
        s = _dot_nt(q, kn)
        key_j = lax.broadcasted_iota(jnp.int32, (rows, PAGE_SIZE), 1)
        q_t = lax.broadcasted_iota(jnp.int32, (rows, PAGE_SIZE), 0) % DEC_SEQ
        s = jnp.where(key_j <= q_t, s, NEG_BIG)
        m = jnp.max(s, axis=-1, keepdims=True)
        p = jnp.exp(s - m)
        m_ref[...] = m
        l_ref[...] = jnp.sum(p, axis=-1, keepdims=True)
        acc_ref[...] = _dot(p.astype(BF16), kn[:, 0:KV_RANK])

    q_next = jnp.where(c == nch - 1, qn_ref[...].reshape(rows, QK_WIDTH), q_f32).astype(BF16)

    def attend_and_score_next(p):
        scores(q_next, 1 - p)
        s = s_buf[p]
        m_prev = m_ref[...]
        m_new = jnp.maximum(m_prev, jnp.max(s, axis=-1, keepdims=True))
        alpha = jnp.exp(m_prev - m_new)
        prob = jnp.exp(s - m_new)
        l_ref[...] = alpha * l_ref[...] + jnp.sum(prob, axis=-1, keepdims=True)
        acc_ref[...] = alpha * acc_ref[...] + _dot(prob.astype(BF16), kcb_buf[p])
        m_ref[...] = m_new

    @pl.when(parity == 0)
    def _():
        attend_and_score_next(0)

    @pl.when(parity == 1)
    def _():
        attend_and_score_next(1)

    @pl.when(c == nch - 1)
    def _():
        o_ref[...] = (acc_ref[...] / l_ref[...]).reshape(N_HEADS, DEC_SEQ, KV_RANK)


def _attn_sample(page_table, q, knew, cache_ckv, cache_krope_t):
    nseq = q.shape[0]
    npg = PAGES_PER_STEP
    rows = N_HEADS * DEC_SEQ
    keys = npg * PAGE_SIZE
    q_block = (None, N_HEADS, DEC_SEQ, QK_WIDTH)
    grid_spec = pltpu.PrefetchScalarGridSpec(
        num_scalar_prefetch=1,
        grid=(nseq, N_PAGES // npg),
        in_specs=[pl.BlockSpec(q_block, lambda s, c, pt: (s, 0, 0, 0)),
                  pl.BlockSpec(q_block, lambda s, c, pt: (jnp.minimum(s + 1, nseq - 1), 0, 0, 0)),
                  pl.BlockSpec((DEC_SEQ, QK_WIDTH), lambda s, c, pt: (s, 0)),
                  pl.BlockSpec(memory_space=pl.ANY),
                  pl.BlockSpec(memory_space=pl.ANY)],
        out_specs=pl.BlockSpec((None, N_HEADS, DEC_SEQ, KV_RANK), lambda s, c, pt: (s, 0, 0, 0)),
        scratch_shapes=[pltpu.VMEM((2, keys, KV_RANK), F32),
                        pltpu.VMEM((2, npg, QK_ROPE_DIM, PAGE_SIZE), F32),
                        pltpu.SemaphoreType.DMA((2,)),
                        pltpu.VMEM((2, rows, keys), F32),
                        pltpu.VMEM((2, keys, KV_RANK), BF16),
                        pltpu.VMEM((rows, 1), F32), pltpu.VMEM((rows, 1), F32),
                        pltpu.VMEM((rows, KV_RANK), F32)],
    )
    return pl.pallas_call(
        _attn_sample_kernel,
        grid_spec=grid_spec,
        out_shape=jax.ShapeDtypeStruct((nseq, N_HEADS, DEC_SEQ, KV_RANK), F32),
        compiler_params=_params(2),
        name="attn_sample",
    )(page_table, q, q, knew, cache_ckv, cache_krope_t)


def _vup_sample_kernel(o_ref, wvb_ref, mla_ref):
    ns = PROJ_ROWS // DEC_SEQ
    for hh in range(N_HEADS):
        o_h = o_ref[:, hh, :, :].reshape(ns * DEC_SEQ, KV_RANK).astype(BF16)
        mla_ref[:, hh * V_HEAD_DIM:(hh + 1) * V_HEAD_DIM] = _dot(o_h, wvb_ref[hh]).astype(BF16)


def _vup_sample(o_lat, wvb):
    nseq = o_lat.shape[0]
    ns = PROJ_ROWS // DEC_SEQ
    return pl.pallas_call(
        _vup_sample_kernel,
        grid=(nseq // ns,),
        in_specs=[pl.BlockSpec((ns, N_HEADS, DEC_SEQ, KV_RANK), lambda i: (i, 0, 0, 0)),
                  _const_spec(wvb.shape)],
        out_specs=pl.BlockSpec((PROJ_ROWS, MLA_WIDTH), lambda i: (i, 0)),
        out_shape=jax.ShapeDtypeStruct((nseq * DEC_SEQ, MLA_WIDTH), BF16),
        compiler_params=_params(1),
        name="vup_sample",
    )(o_lat, wvb)


def _rope_tables(pos):
    inv = ROPE_THETA ** (-jnp.arange(0, QK_ROPE_DIM, 2, dtype=F32) / QK_ROPE_DIM)
    ang = pos.astype(F32)[:, None] * inv[None, :]
    cos, sin = jnp.cos(ang), jnp.sin(ang)
    reps = ROPE_PAD // QK_ROPE_DIM
    cs = jnp.tile(jnp.concatenate([cos, cos], axis=-1), (1, reps))
    sn = jnp.tile(jnp.concatenate([-sin, sin], axis=-1), (1, reps))
    return cs, sn


def _swap_halves(w):
    half = w.shape[-1] // 2
    return jnp.concatenate([w[..., half:], w[..., :half]], axis=-1)


def _pad_lanes(w, width):
    return jnp.pad(w, [(0, 0)] * (w.ndim - 1) + [(0, width - w.shape[-1])])


def _layout_weights(attn_norm, w_in, q_a_norm, w_q_b, kv_a_norm, w_k_b, w_pool, pool_scale):
    s1 = Q_RANK + KV_RANK
    s2 = s1 + QK_ROPE_DIM
    w_t = jnp.swapaxes(w_in, 0, 1)
    w_kr = w_t[s1:s2]
    half = QK_ROPE_DIM // 2
    zeros = jnp.zeros((ROPE_PAD - QK_ROPE_DIM, D_MODEL), w_in.dtype)
    w_rope = jnp.concatenate([w_kr, zeros, w_kr[half:], w_kr[:half], zeros], axis=0).astype(BF16)
    w_t = w_t.astype(BF16)
    wq_nope = w_q_b[:, :, :QK_NOPE_DIM].reshape(Q_RANK, Q_NOPE_W)
    wq_r = w_q_b[:, :, QK_NOPE_DIM:]
    wq_rope = _pad_lanes(wq_r, ROPE_PAD).reshape(Q_RANK, Q_ROPE_W)
    wq_swap = _pad_lanes(_swap_halves(wq_r), ROPE_PAD).reshape(Q_RANK, Q_ROPE_W)
    wq = jnp.concatenate([wq_nope, wq_rope, wq_swap], axis=1).astype(BF16)
    wkb = jnp.transpose(w_k_b, (1, 2, 0)).astype(BF16)
    zero = jnp.zeros_like(wkb[0::2])
    wkb = jnp.concatenate([jnp.concatenate([wkb[0::2], zero], axis=2),
                           jnp.concatenate([zero, wkb[1::2]], axis=2)], axis=1)
    return (attn_norm.reshape(1, D_MODEL), w_t, w_rope, q_a_norm.reshape(1, Q_RANK), wq,
            kv_a_norm.reshape(1, KV_RANK), wkb, w_pool.astype(BF16),
            pool_scale.reshape(1, POOL_WIDTH))


def kernel(x_prompt, x_sample, mem_prompt, cache_ckv, cache_krope, page_table, state_pool,
           cache_mem_k, cache_mem_v, attn_norm, mem_norm, w_in, q_a_norm, w_q_b, kv_a_norm,
           w_k_b, w_v_b, w_pool, pool_scale, w_mem_kv, w_out, final_norm):
    assert DEPTH == 1 and attn_norm.shape[0] == 1
    n_phys = cache_ckv.shape[1]
    wts = _layout_weights(attn_norm[0], w_in[0], q_a_norm[0], w_q_b[0], kv_a_norm[0], w_k_b[0],
                          w_pool[0], pool_scale[0])
    wvb = jnp.transpose(w_v_b[0], (1, 0, 2)).astype(BF16)
    wout = w_out[0].astype(BF16)
    fn = final_norm.reshape(1, D_MODEL)

    mk, mv, mkb, mvb = _mem_kv(mem_prompt, mem_norm[0].reshape(1, D_MODEL), w_mem_kv[0].astype(BF16))
    cs_p, sn_p = _rope_tables(jnp.arange(SEQ))
    q_p, ckv_p, kr_p, kc_p, pool_p, cross_p, sg_p, pst_p = _proj_prompt(
        x_prompt, cs_p, sn_p, wts, mkb, mvb)
    mla_p = _attn_prompt(q_p, kc_p, wvb)
    rows_p = BATCH * SEQ
    y_p = _combine(x_prompt.reshape(rows_p, D_MODEL), mla_p.reshape(rows_p, MLA_WIDTH),
                   pool_p.reshape(rows_p, POOL_WIDTH), cross_p.reshape(rows_p, MEM_WIDTH),
                   sg_p.reshape(rows_p, D_MIX), wout, fn)

    rows_s = DEC_BATCH * DEC_SEQ
    cs_s, sn_s = _rope_tables(PAST_LEN + jnp.arange(DEC_SEQ))
    xs = x_sample.reshape(rows_s, D_MODEL)
    q_s, ckv_s, kr_s, kc_s, pool_s, qm_s, sg_s, pst_s = _proj_sample(
        xs, cs_s, sn_s, wts, state_pool[0])
    cross_s = _cross_sample(
        qm_s,
        cache_mem_k.reshape(DEC_BATCH, MEM_TOKENS * MEM_HEADS, MEM_HEAD_DIM),
        cache_mem_v.reshape(DEC_BATCH, MEM_TOKENS * MEM_HEADS, MEM_HEAD_DIM))
    o_lat_s = _attn_sample(page_table, q_s, kc_s,
                           cache_ckv.reshape(n_phys, PAGE_SIZE, KV_RANK),
                           jnp.swapaxes(cache_krope.reshape(n_phys, PAGE_SIZE, QK_ROPE_DIM), 1, 2))
    mla_s = _vup_sample(o_lat_s, wvb)
    y_s = _combine(xs, mla_s, pool_s, cross_s, sg_s, wout, fn)

    return (y_p.reshape(BATCH, SEQ, D_MODEL),
            y_s.reshape(DEC_BATCH, DEC_SEQ, D_MODEL),
            ckv_p[None],
            kr_p[None],
            pst_p[None],
            mk.reshape(1, BATCH, MEM_TOKENS, MEM_HEADS, MEM_HEAD_DIM),
            mv.reshape(1, BATCH, MEM_TOKENS, MEM_HEADS, MEM_HEAD_DIM),
            ckv_s.reshape(1, DEC_BATCH, DEC_SEQ, KV_RANK),
            kr_s.reshape(1, DEC_BATCH, DEC_SEQ, QK_ROPE_DIM),
            pst_s[None])
```

```python
import functools

import jax
import jax.numpy as jnp
from jax import lax
from jax.experimental import pallas as pl
from jax.experimental.pallas import tpu as pltpu

F32 = jnp.float32
BF16 = jnp.bfloat16

D_MODEL = 2048
BATCH = 2
SEQ = 4096
DEPTH = 1
DEC_BATCH = 128
DEC_SEQ = 8
PAST_LEN = 16384
PAGE_SIZE = 128
N_PAGES = PAST_LEN // PAGE_SIZE

N_HEADS = 8
QK_NOPE_DIM = 128
QK_ROPE_DIM = 64
V_HEAD_DIM = 128
Q_RANK = 512
KV_RANK = 256
MLA_WIDTH = N_HEADS * V_HEAD_DIM
MLA_SCALE = (QK_NOPE_DIM + QK_ROPE_DIM) ** -0.5
ROPE_THETA = 10000.0
POOL_GROUPS = 4
POOL_WINDOWS = (2, 4, 8, 16)
POOL_GROUP_DIM = 128
POOL_WIDTH = POOL_GROUPS * POOL_GROUP_DIM
POOL_STATE_LEN = 15
MEM_TOKENS = 256
MEM_HEADS = 4
MEM_HEAD_DIM = 128
MEM_WIDTH = MEM_HEADS * MEM_HEAD_DIM
MEM_SCALE = MEM_HEAD_DIM ** -0.5
D_MIX = MLA_WIDTH + POOL_WIDTH + MEM_WIDTH
RMS_EPS = 1e-6

LANE = 128
ROPE_PAD = LANE
QK_WIDTH = KV_RANK + ROPE_PAD
C_QA = 0
C_CKV = C_QA + Q_RANK
C_KR = C_CKV + KV_RANK
C_U = C_KR + 2 * ROPE_PAD
C_QM = C_U + POOL_WIDTH
C_GATE = C_QM + MEM_WIDTH
C_END = C_GATE + D_MIX
Q_NOPE_W = N_HEADS * QK_NOPE_DIM
Q_ROPE_W = N_HEADS * ROPE_PAD

VMEM_LIMIT = 56 * 1024 * 1024
NEG_BIG = -1e30

PROJ_ROWS = 256
GATE_PARTS = 4
ATT_Q = 256
ATT_K = 512
ATT_SUB_HEADS = 2
PAGES_PER_STEP = 32
CROSS_SEQS = 8


def _const_spec(shape):
    n = len(shape)
    return pl.BlockSpec(shape, lambda *_: (0,) * n, pipeline_mode=pl.Buffered(1))


def _params(n_axes):
    return pltpu.CompilerParams(dimension_semantics=("arbitrary",) * n_axes,
                                vmem_limit_bytes=VMEM_LIMIT)


def _rms(x, g):
    return x * lax.rsqrt(jnp.mean(x * x, axis=-1, keepdims=True) + RMS_EPS) * g


def _dot(a, b):
    return jnp.dot(a, b, preferred_element_type=F32)


def _dot_nt(a, b):
    return lax.dot_general(a, b, (((1,), (1,)), ((), ())), preferred_element_type=F32)


def _softmax_rows(s):
    m = jnp.max(s, axis=-1, keepdims=True)
    e = jnp.exp(s - m)
    return e / jnp.sum(e, axis=-1, keepdims=True)


def _mem_kv_kernel(mem_ref, g_ref, w_ref, k_ref, v_ref, kb_ref, vb_ref):
    h = _rms(mem_ref[...], g_ref[...]).astype(BF16)
    kv = _dot(h, w_ref[...])
    k = kv[:, :MEM_WIDTH]
    v = kv[:, MEM_WIDTH:]
    k_ref[...] = k
    v_ref[...] = v
    kb_ref[...] = k.astype(BF16)
    vb_ref[...] = v.astype(BF16)


def _mem_kv(mem, mem_norm, w_mem_kv):
    b = mem.shape[0]
    out_f32 = jax.ShapeDtypeStruct((b, MEM_TOKENS, MEM_WIDTH), F32)
    out_bf16 = jax.ShapeDtypeStruct((b, MEM_TOKENS, MEM_WIDTH), BF16)
    blk = pl.BlockSpec((None, MEM_TOKENS, MEM_WIDTH), lambda i: (i, 0, 0))
    return pl.pallas_call(
        _mem_kv_kernel,
        grid=(b,),
        in_specs=[pl.BlockSpec((None, MEM_TOKENS, D_MODEL), lambda i: (i, 0, 0)),
                  _const_spec((1, D_MODEL)),
                  _const_spec((D_MODEL, 2 * MEM_WIDTH))],
        out_specs=[blk, blk, blk, blk],
        out_shape=[out_f32, out_f32, out_bf16, out_bf16],
        compiler_params=_params(1),
        name="mem_kv",
    )(mem, mem_norm, w_mem_kv)


def _normed_input(x, an_ref):
    return _rms(x, an_ref[...]).astype(BF16)


def _segment(h, w1_ref, a, b):
    wt_ref, wrope_ref = w1_ref
    rope_end = Q_RANK + KV_RANK + QK_ROPE_DIM
    if b <= C_KR:
        return _dot_nt(h, wt_ref[a:b, :])
    if C_KR <= a and b <= C_U:
        return _dot_nt(h, wrope_ref[a - C_KR:b - C_KR, :])
    if C_U <= a:
        return _dot_nt(h, wt_ref[a - C_U + rope_end:b - C_U + rope_end, :])
    raise ValueError(f"columns [{a}, {b}) straddle two weight groups")


def _query_low_rank(h, w1_ref, qan_ref):
    return _rms(_segment(h, w1_ref, C_QA, C_CKV), qan_ref[...]).astype(BF16)


def _query_up(qa, wq_ref):
    q_nope = _dot(qa, wq_ref[:, 0:Q_NOPE_W])
    q_rope = _dot(qa, wq_ref[:, Q_NOPE_W:Q_NOPE_W + Q_ROPE_W])
    q_swap = _dot(qa, wq_ref[:, Q_NOPE_W + Q_ROPE_W:Q_NOPE_W + 2 * Q_ROPE_W])
    return q_nope, q_rope, q_swap


def _query_heads(q_up, cs, sn, wkb_ref):
    q_nope, q_rope, q_swap = q_up
    heads = []
    for pair in range(N_HEADS // 2):
        both = slice(2 * pair * QK_NOPE_DIM, 2 * (pair + 1) * QK_NOPE_DIM)
        q_lat2 = _dot(q_nope[:, both].astype(BF16), wkb_ref[pair]) * MLA_SCALE
        for k in range(2):
            sl = slice((2 * pair + k) * LANE, (2 * pair + k + 1) * LANE)
            q_r = (q_rope[:, sl] * cs + q_swap[:, sl] * sn) * MLA_SCALE
            heads.append((q_lat2[:, k * KV_RANK:(k + 1) * KV_RANK], q_r))
    return heads


def _latent_key(h, cs, sn, w1_ref, kvn_ref):
    ckv = _rms(_segment(h, w1_ref, C_CKV, C_KR), kvn_ref[...])
    zk = _segment(h, w1_ref, C_KR, C_U)
    kr = zk[:, :ROPE_PAD] * cs + zk[:, ROPE_PAD:] * sn
    return ckv, kr


def _silu_gate_part(h, w1_ref, sg_ref, part):
    width = D_MIX // GATE_PARTS
    g = _segment(h, w1_ref, C_GATE + part * width, C_GATE + (part + 1) * width)
    sg_ref[:, part * width:(part + 1) * width] = (g / (1.0 + jnp.exp(-g))).astype(BF16)


def _proj_prompt_kernel(x_ref, cs_ref, sn_ref, an_ref, wt_ref, wrope_ref, qan_ref, wq_ref,
                        kvn_ref, wkb_ref, wpool_ref, pscale_ref, mk_ref, mv_ref,
                        q_ref, ckv_ref, kr_ref, kc_ref, pool_ref, cross_ref, sg_ref, pst_ref,
                        uext_ref):
    w1_ref = (wt_ref, wrope_ref)
    j = pl.program_id(1)
    tm = PROJ_ROWS
    halo = 16

    @pl.when(j == 0)
    def _():
        uext_ref[0:halo, :] = jnp.zeros((halo, POOL_WIDTH), F32)

    cs = cs_ref[...]
    sn = sn_ref[...]
    h = _normed_input(x_ref[...], an_ref)
    u_qm = _segment(h, w1_ref, C_U, C_GATE)
    u = u_qm[:, 0:POOL_WIDTH]
    qm = u_qm[:, POOL_WIDTH:]
    qa = _query_low_rank(h, w1_ref, qan_ref)
    _silu_gate_part(h, w1_ref, sg_ref, 0)

    head_lanes = [slice(hh * MEM_HEAD_DIM, (hh + 1) * MEM_HEAD_DIM) for hh in range(MEM_HEADS)]
    scores = [_dot_nt(qm[:, sl].astype(BF16), mk_ref[:, sl]) for sl in head_lanes]
    probs = [_softmax_rows(s * MEM_SCALE).astype(BF16) for s in scores]
    for sl, p in zip(head_lanes, probs):
        cross_ref[:, sl] = _dot(p, mv_ref[:, sl]).astype(BF16)
    q_up = _query_up(qa, wq_ref)
    _silu_gate_part(h, w1_ref, sg_ref, 1)

    uext_ref[halo:halo + tm, :] = u
    pos = j * tm + lax.broadcasted_iota(jnp.int32, (tm, 1), 0)
    group_lanes = [slice(g * POOL_GROUP_DIM, (g + 1) * POOL_GROUP_DIM) for g in range(POOL_GROUPS)]
    diffs = []
    for sl, w in zip(group_lanes, POOL_WINDOWS):
        acc = u[:, sl]
        for k in range(1, w):
            acc = acc + uext_ref[halo - k:halo - k + tm, sl]
        cnt = jnp.minimum(pos + 1, w).astype(F32)
        diffs.append((acc / cnt - u[:, sl]).astype(BF16))
    for g, (sl, diff) in enumerate(zip(group_lanes, diffs)):
        pool_ref[:, sl] = (_dot(diff, wpool_ref[g]) * pscale_ref[:, sl]).astype(BF16)
    uext_ref[0:halo, :] = uext_ref[tm:tm + halo, :]
    _silu_gate_part(h, w1_ref, sg_ref, 2)

    for hh, (q_lat, q_r) in enumerate(_query_heads(q_up, cs, sn, wkb_ref)):
        q_ref[hh, :, 0:KV_RANK] = q_lat.astype(BF16)
        q_ref[hh, :, KV_RANK:QK_WIDTH] = q_r.astype(BF16)
    _silu_gate_part(h, w1_ref, sg_ref, 3)
    ckv, kr = _latent_key(h, cs, sn, w1_ref, kvn_ref)
    ckv_ref[...] = ckv
    kr_ref[...] = kr[:, :QK_ROPE_DIM]
    kc_ref[:, 0:KV_RANK] = ckv.astype(BF16)
    kc_ref[:, KV_RANK:QK_WIDTH] = kr.astype(BF16)

    @pl.when(j == pl.num_programs(1) - 1)
    def _():
        pst_ref[...] = uext_ref[halo + tm - POOL_STATE_LEN:halo + tm, :]


def _proj_prompt(x, cs, sn, wts, mkb, mvb):
    b, t, _ = x.shape
    tm = PROJ_ROWS
    nj = t // tm

    def rows(width, dtype):
        return (pl.BlockSpec((None, tm, width), lambda i, j: (i, j, 0)),
                jax.ShapeDtypeStruct((b, t, width), dtype))

    outs = [
        (pl.BlockSpec((None, N_HEADS, tm, QK_WIDTH), lambda i, j: (i, 0, j, 0)),
         jax.ShapeDtypeStruct((b, N_HEADS, t, QK_WIDTH), BF16)),
        rows(KV_RANK, F32),
        rows(QK_ROPE_DIM, F32),
        rows(QK_WIDTH, BF16),
        rows(POOL_WIDTH, BF16),
        rows(MEM_WIDTH, BF16),
        rows(D_MIX, BF16),
        (pl.BlockSpec((None, POOL_STATE_LEN, POOL_WIDTH), lambda i, j: (i, 0, 0)),
         jax.ShapeDtypeStruct((b, POOL_STATE_LEN, POOL_WIDTH), F32)),
    ]
    mem_spec = pl.BlockSpec((None, MEM_TOKENS, MEM_WIDTH), lambda i, j: (i, 0, 0))
    in_specs = [
        pl.BlockSpec((None, tm, D_MODEL), lambda i, j: (i, j, 0)),
        pl.BlockSpec((tm, ROPE_PAD), lambda i, j: (j, 0)),
        pl.BlockSpec((tm, ROPE_PAD), lambda i, j: (j, 0)),
    ] + [_const_spec(w.shape) for w in wts] + [mem_spec, mem_spec]
    return pl.pallas_call(
        _proj_prompt_kernel,
        grid=(b, nj),
        in_specs=in_specs,
        out_specs=[o[0] for o in outs],
        out_shape=[o[1] for o in outs],
        scratch_shapes=[pltpu.VMEM((16 + tm, POOL_WIDTH), F32)],
        compiler_params=_params(2),
        name="proj_prompt",
    )(x, cs, sn, *wts, mkb, mvb)


def _attn_prompt_kernel(q_ref, kc_ref, wvb_ref, o_ref, s_even, s_odd, m_ref, l_ref, acc_ref):
    i = pl.program_id(1)
    rows = N_HEADS * ATT_Q
    sub = ATT_SUB_HEADS * ATT_Q
    n_sub = N_HEADS // ATT_SUB_HEADS
    m_ref[...] = jnp.full((rows, 1), NEG_BIG, F32)
    l_ref[...] = jnp.zeros((rows, 1), F32)
    acc_ref[...] = jnp.zeros((rows, KV_RANK), F32)

    def keys(c):
        return kc_ref[pl.ds(pl.multiple_of(c * ATT_K, ATT_K), ATT_K), :]

    def scores(c, s_ref):
        k = keys(c)
        for sb in range(n_sub):
            q = q_ref[sb * ATT_SUB_HEADS:(sb + 1) * ATT_SUB_HEADS].reshape(sub, QK_WIDTH)
            s_ref[sb * sub:(sb + 1) * sub, :] = _dot_nt(q, k)

    def attend(c, s_ref, diag_width=None):
        masked = diag_width is not None
        width = diag_width if masked else ATT_K
        v = kc_ref[pl.ds(pl.multiple_of(c * ATT_K, ATT_K), width), 0:KV_RANK]
        if masked:
            key_pos = c * ATT_K + lax.broadcasted_iota(jnp.int32, (ATT_Q, width), 1)
            q_pos = i * ATT_Q + lax.broadcasted_iota(jnp.int32, (ATT_Q, width), 0)
            visible = (key_pos <= q_pos)[None]
        for sb in range(n_sub):
            r = slice(sb * sub, (sb + 1) * sub)
            s = s_ref[r, 0:width]
            if masked:
                s = jnp.where(visible, s.reshape(ATT_SUB_HEADS, ATT_Q, width), NEG_BIG)
                s = s.reshape(sub, width)
            m_prev = m_ref[r, :]
            m_new = jnp.maximum(m_prev, jnp.max(s, axis=-1, keepdims=True))
            alpha = jnp.exp(m_prev - m_new)
            p = jnp.exp(s - m_new)
            l_ref[r, :] = alpha * l_ref[r, :] + jnp.sum(p, axis=-1, keepdims=True)
            acc_ref[r, :] = alpha * acc_ref[r, :] + _dot(p.astype(BF16), v)
            m_ref[r, :] = m_new

    n_full = (i * ATT_Q) // ATT_K
    scores(0, s_even)

    def body(c, carry):
        @pl.when(lax.rem(c, 2) == 0)
        def _():
            scores(c + 1, s_odd)
            attend(c, s_even)

        @pl.when(lax.rem(c, 2) == 1)
        def _():
            scores(c + 1, s_even)
            attend(c, s_odd)

        return carry

    lax.fori_loop(0, n_full, body, 0)

    blocks_per_chunk = ATT_K // ATT_Q
    for parity, s_ref in enumerate((s_even, s_odd)):
        for d in range(blocks_per_chunk):
            @pl.when(jnp.logical_and(lax.rem(n_full, 2) == parity, lax.rem(i, blocks_per_chunk) == d))
            def _(s_ref=s_ref, d=d):
                attend(n_full, s_ref, diag_width=(d + 1) * ATT_Q)

    for hh in range(N_HEADS):
        r = slice(hh * ATT_Q, (hh + 1) * ATT_Q)
        o_h = (acc_ref[r, :] / l_ref[r, :]).astype(BF16)
        o_ref[:, hh * V_HEAD_DIM:(hh + 1) * V_HEAD_DIM] = _dot(o_h, wvb_ref[hh]).astype(BF16)


def _attn_prompt(q, kc, wvb):
    b, _, t, _ = q.shape
    rows = N_HEADS * ATT_Q
    return pl.pallas_call(
        _attn_prompt_kernel,
        grid=(b, t // ATT_Q),
        in_specs=[pl.BlockSpec((None, N_HEADS, ATT_Q, QK_WIDTH), lambda bi, i: (bi, 0, i, 0)),
                  pl.BlockSpec((None, t, QK_WIDTH), lambda bi, i: (bi, 0, 0)),
                  _const_spec(wvb.shape)],
        out_specs=pl.BlockSpec((None, ATT_Q, MLA_WIDTH), lambda bi, i: (bi, i, 0)),
        out_shape=jax.ShapeDtypeStruct((b, t, MLA_WIDTH), BF16),
        scratch_shapes=[pltpu.VMEM((rows, ATT_K), F32), pltpu.VMEM((rows, ATT_K), F32),
                        pltpu.VMEM((rows, 1), F32), pltpu.VMEM((rows, 1), F32),
                        pltpu.VMEM((rows, KV_RANK), F32)],
        compiler_params=_params(2),
        name="attn_prompt",
    )(q, kc, wvb)


def _combine_kernel(x_ref, mla_ref, pool_ref, cross_ref, sg_ref, wout_ref, fn_ref, y_ref):
    def part(ref, a, b):
        mix = ref[...].astype(F32) * sg_ref[:, a:b].astype(F32)
        return _dot(mix.astype(BF16), wout_ref[a:b, :])

    acc = x_ref[...] + part(mla_ref, 0, MLA_WIDTH)
    acc = acc + part(pool_ref, MLA_WIDTH, MLA_WIDTH + POOL_WIDTH)
    acc = acc + part(cross_ref, MLA_WIDTH + POOL_WIDTH, D_MIX)
    y_ref[...] = _rms(acc, fn_ref[...])


def _combine(x, mla, pool, cross, sg, wout, fn):
    n = x.shape[0]
    tm = PROJ_ROWS

    def rows(width):
        return pl.BlockSpec((tm, width), lambda i: (i, 0))

    return pl.pallas_call(
        _combine_kernel,
        grid=(n // tm,),
        in_specs=[rows(D_MODEL), rows(MLA_WIDTH), rows(POOL_WIDTH), rows(MEM_WIDTH), rows(D_MIX),
                  _const_spec(wout.shape), _const_spec(fn.shape)],
        out_specs=rows(D_MODEL),
        out_shape=jax.ShapeDtypeStruct((n, D_MODEL), F32),
        compiler_params=_params(1),
        name="combine",
    )(x, mla, pool, cross, sg, wout, fn)


def _proj_sample_kernel(x_ref, cs_ref, sn_ref, an_ref, wt_ref, wrope_ref, qan_ref, wq_ref,
                        kvn_ref, wkb_ref, wpool_ref, pscale_ref, st_ref,
                        q_ref, ckv_ref, kr_ref, kc_ref, pool_ref, qm_ref, sg_ref, pst_ref,
                        ext_ref):
    w1_ref = (wt_ref, wrope_ref)
    tm = PROJ_ROWS
    ns = tm // DEC_SEQ
    cs = jnp.broadcast_to(cs_ref[...][None], (ns, DEC_SEQ, ROPE_PAD)).reshape(tm, ROPE_PAD)
    sn = jnp.broadcast_to(sn_ref[...][None], (ns, DEC_SEQ, ROPE_PAD)).reshape(tm, ROPE_PAD)
    h = _normed_input(x_ref[...], an_ref)
    u_qm = _segment(h, w1_ref, C_U, C_GATE)
    u = u_qm[:, 0:POOL_WIDTH]
    qm_ref[...] = u_qm[:, POOL_WIDTH:]
    q_up = _query_up(_query_low_rank(h, w1_ref, qan_ref), wq_ref)
    for hh, (q_lat, q_r) in enumerate(_query_heads(q_up, cs, sn, wkb_ref)):
        q_ref[:, hh, :, 0:KV_RANK] = q_lat.reshape(ns, DEC_SEQ, KV_RANK)
        q_ref[:, hh, :, KV_RANK:QK_WIDTH] = q_r.reshape(ns, DEC_SEQ, ROPE_PAD)
    ckv, kr = _latent_key(h, cs, sn, w1_ref, kvn_ref)
    ckv_ref[...] = ckv
    kr_ref[...] = kr[:, :QK_ROPE_DIM]
    kc_ref[:, 0:KV_RANK] = ckv
    kc_ref[:, KV_RANK:QK_WIDTH] = kr
    for part in range(GATE_PARTS):
        _silu_gate_part(h, w1_ref, sg_ref, part)

    base = POOL_STATE_LEN + 1
    ext_ref[:, 1:base, :] = st_ref[...]
    ext_ref[:, base:base + DEC_SEQ, :] = u.reshape(ns, DEC_SEQ, POOL_WIDTH)
    for g, w in enumerate(POOL_WINDOWS):
        sl = slice(g * POOL_GROUP_DIM, (g + 1) * POOL_GROUP_DIM)
        acc = ext_ref[:, base:base + DEC_SEQ, sl]
        for k in range(1, w):
            acc = acc + ext_ref[:, base - k:base - k + DEC_SEQ, sl]
        diff = (acc * (1.0 / w)).reshape(tm, POOL_GROUP_DIM) - u[:, sl]
        y = _dot(diff.astype(BF16), wpool_ref[g]) * pscale_ref[:, sl]
        pool_ref[:, sl] = y.astype(BF16)
    pst_ref[...] = ext_ref[:, base + DEC_SEQ - POOL_STATE_LEN:base + DEC_SEQ, :]


def _proj_sample(x, cs, sn, wts, state):
    n = x.shape[0]
    tm = PROJ_ROWS
    ns = tm // DEC_SEQ
    nseq = n // DEC_SEQ

    def rows(width, dtype):
        return (pl.BlockSpec((tm, width), lambda i: (i, 0)), jax.ShapeDtypeStruct((n, width), dtype))

    outs = [
        (pl.BlockSpec((ns, N_HEADS, DEC_SEQ, QK_WIDTH), lambda i: (i, 0, 0, 0)),
         jax.ShapeDtypeStruct((nseq, N_HEADS, DEC_SEQ, QK_WIDTH), F32)),
        rows(KV_RANK, F32),
        rows(QK_ROPE_DIM, F32),
        rows(QK_WIDTH, F32),
        rows(POOL_WIDTH, BF16),
        rows(MEM_WIDTH, F32),
        rows(D_MIX, BF16),
        (pl.BlockSpec((ns, POOL_STATE_LEN, POOL_WIDTH), lambda i: (i, 0, 0)),
         jax.ShapeDtypeStruct((nseq, POOL_STATE_LEN, POOL_WIDTH), F32)),
    ]
    in_specs = [
        pl.BlockSpec((tm, D_MODEL), lambda i: (i, 0)),
        _const_spec(cs.shape),
        _const_spec(sn.shape),
    ] + [_const_spec(w.shape) for w in wts] + [
        pl.BlockSpec((ns, POOL_STATE_LEN, POOL_WIDTH), lambda i: (i, 0, 0))]
    return pl.pallas_call(
        _proj_sample_kernel,
        grid=(n // tm,),
        in_specs=in_specs,
        out_specs=[o[0] for o in outs],
        out_shape=[o[1] for o in outs],
        scratch_shapes=[pltpu.VMEM((ns, POOL_STATE_LEN + 1 + DEC_SEQ, POOL_WIDTH), F32)],
        compiler_params=_params(1),
        name="proj_sample",
    )(x, cs, sn, *wts, state)


def _cross_sample_kernel(qm_ref, k_ref, v_ref, o_ref):
    rows = MEM_HEADS * DEC_SEQ
    cols = MEM_TOKENS * MEM_HEADS
    r_head = lax.broadcasted_iota(jnp.int32, (rows, cols), 0) // DEC_SEQ
    c_head = lax.broadcasted_iota(jnp.int32, (rows, cols), 1) % MEM_HEADS
    same_head = r_head == c_head
    scores = []
    for s_i in range(CROSS_SEQS):
        q = qm_ref[s_i * DEC_SEQ:(s_i + 1) * DEC_SEQ, :]
        q_ht = jnp.concatenate([q[:, hh * MEM_HEAD_DIM:(hh + 1) * MEM_HEAD_DIM]
                                for hh in range(MEM_HEADS)], axis=0).astype(BF16)
        scores.append(_dot_nt(q_ht, k_ref[s_i].astype(BF16)))
    probs = [_softmax_rows(jnp.where(same_head, s * MEM_SCALE, NEG_BIG)).astype(BF16) for s in scores]
    outs = []
    for s_i in range(CROSS_SEQS):
        o_ht = _dot(probs[s_i], v_ref[s_i].astype(BF16))
        outs.append(jnp.concatenate([o_ht[hh * DEC_SEQ:(hh + 1) * DEC_SEQ]
                                     for hh in range(MEM_HEADS)], axis=-1))
    o_ref[...] = jnp.concatenate(outs, axis=0).astype(BF16)


def _cross_sample(qm, mem_k, mem_v):
    n = qm.shape[0]
    tm = CROSS_SEQS * DEC_SEQ
    kv_spec = pl.BlockSpec((CROSS_SEQS, MEM_TOKENS * MEM_HEADS, MEM_HEAD_DIM), lambda i: (i, 0, 0))
    return pl.pallas_call(
        _cross_sample_kernel,
        grid=(n // tm,),
        in_specs=[pl.BlockSpec((tm, MEM_WIDTH), lambda i: (i, 0)), kv_spec, kv_spec],
        out_specs=pl.BlockSpec((tm, MEM_WIDTH), lambda i: (i, 0)),
        out_shape=jax.ShapeDtypeStruct((n, MEM_WIDTH), BF16),
        compiler_params=_params(1),
        name="cross_sample",
    )(qm, mem_k, mem_v)


def _page_copies(pages, ckv_hbm, krt_hbm, ckv_buf, krt_buf, sem, slot):
    copies = []
    for j, pg in enumerate(pages):
        copies.append(pltpu.make_async_copy(
            ckv_hbm.at[pg], ckv_buf.at[slot, pl.ds(j * PAGE_SIZE, PAGE_SIZE), :], sem.at[slot]))
        copies.append(pltpu.make_async_copy(krt_hbm.at[pg], krt_buf.at[slot, j], sem.at[slot]))
    return copies


def _attn_sample_kernel(pt_ref, q_ref, qn_ref, knew_ref, ckv_hbm, krt_hbm, o_ref,
                        ckv_buf, krt_buf, sem, s_buf, kcb_buf, m_ref, l_ref, acc_ref):
    npg = PAGES_PER_STEP
    nch = N_PAGES // npg
    s_i = pl.program_id(0)
    c = pl.program_id(1)
    step = s_i * nch + c
    last_step = pl.num_programs(0) * nch - 1
    parity = lax.rem(step, 2)
    rows = N_HEADS * DEC_SEQ

    def advance(seq, chunk):
        wrap = chunk + 1 == nch
        return jnp.where(wrap, seq + 1, seq), jnp.where(wrap, 0, chunk + 1)

    def start_fetch(seq, chunk, slot):
        pages = [pt_ref[seq, chunk * npg + j] for j in range(npg)]
        for n, cp in enumerate(_page_copies(pages, ckv_hbm, krt_hbm, ckv_buf, krt_buf, sem, slot)):
            cp.start(priority=(n // 2) % 2)

    def wait(slot):
        for cp in _page_copies([0] * npg, ckv_hbm, krt_hbm, ckv_buf, krt_buf, sem, slot):
            cp.wait()

    def scores(q_rows, p):
        kc = ckv_buf[p].astype(BF16)
        krt = jnp.concatenate([krt_buf[p, j] for j in range(npg)], axis=-1).astype(BF16)
        kcb_buf[p] = kc
        s_buf[p] = (_dot_nt(q_rows[:, 0:KV_RANK], kc)
                    + _dot(q_rows[:, KV_RANK:KV_RANK + QK_ROPE_DIM], krt))

    seq1, chunk1 = advance(s_i, c)
    seq2, chunk2 = advance(seq1, chunk1)
    q_f32 = q_ref[...].reshape(rows, QK_WIDTH)
    q = q_f32.astype(BF16)

    @pl.when(step == 0)
    def _():
        start_fetch(s_i, c, 0)
        start_fetch(seq1, chunk1, 1)
        wait(0)
        scores(q, 0)

    @pl.when(step + 2 <= last_step)
    def _():
        start_fetch(seq2, chunk2, parity)

    @pl.when(step < last_step)
    def _():
        wait(1 - parity)

    @pl.when(c == 0)
    def _():
        pad = jnp.zeros((PAGE_SIZE - DEC_SEQ, QK_WIDTH), F32)
        kn = jnp.concatenate([knew_ref[...], pad], axis=0).astype(BF16)
        s = _dot_nt(q, kn)
        key_j = lax.broadcasted_iota(jnp.int32, (rows, PAGE_SIZE), 1)
        q_t = lax.broadcasted_iota(jnp.int32, (rows, PAGE_SIZE), 0) % DEC_SEQ
        s = jnp.where(key_j <= q_t, s, NEG_BIG)
        m = jnp.max(s, axis=-1, keepdims=True)
        p = jnp.exp(s - m)
        m_ref[...] = m
        l_ref[...] = jnp.sum(p, axis=-1, keepdims=True)
        acc_ref[...] = _dot(p.astype(BF16), kn[:, 0:KV_RANK])

    q_next = jnp.where(c == nch - 1, qn_ref[...].reshape(rows, QK_WIDTH), q_f32).astype(BF16)

    def attend_and_score_next(p):
        scores(q_next, 1 - p)
        s = s_buf[p]
        m_prev = m_ref[...]
        m_new = jnp.maximum(m_prev, jnp.max(s, axis=-1, keepdims=True))
        alpha = jnp.exp(m_prev - m_new)
        prob = jnp.exp(s - m_new)
        l_ref[...] = alpha * l_ref[...] + jnp.sum(prob, axis=-1, keepdims=True)
        acc_ref[...] = alpha * acc_ref[...] + _dot(prob.astype(BF16), kcb_buf[p])
        m_ref[...] = m_new

    @pl.when(parity == 0)
    def _():
        attend_and_score_next(0)

    @pl.when(parity == 1)
    def _():
        attend_and_score_next(1)

    @pl.when(c == nch - 1)
    def _():
        o_ref[...] = (acc_ref[...] / l_ref[...]).reshape(N_HEADS, DEC_SEQ, KV_RANK)


def _attn_sample(page_table, q, knew, cache_ckv, cache_krope_t):
    nseq = q.shape[0]
    npg = PAGES_PER_STEP
    rows = N_HEADS * DEC_SEQ
    keys = npg * PAGE_SIZE
    q_block = (None, N_HEADS, DEC_SEQ, QK_WIDTH)
    grid_spec = pltpu.PrefetchScalarGridSpec(
        num_scalar_prefetch=1,
        grid=(nseq, N_PAGES // npg),
        in_specs=[pl.BlockSpec(q_block, lambda s, c, pt: (s, 0, 0, 0)),
                  pl.BlockSpec(q_block, lambda s, c, pt: (jnp.minimum(s + 1, nseq - 1), 0, 0, 0)),
                  pl.BlockSpec((DEC_SEQ, QK_WIDTH), lambda s, c, pt: (s, 0)),
                  pl.BlockSpec(memory_space=pl.ANY),
                  pl.BlockSpec(memory_space=pl.ANY)],
        out_specs=pl.BlockSpec((None, N_HEADS, DEC_SEQ, KV_RANK), lambda s, c, pt: (s, 0, 0, 0)),
        scratch_shapes=[pltpu.VMEM((2, keys, KV_RANK), F32),
                        pltpu.VMEM((2, npg, QK_ROPE_DIM, PAGE_SIZE), F32),
                        pltpu.SemaphoreType.DMA((2,)),
                        pltpu.VMEM((2, rows, keys), F32),
                        pltpu.VMEM((2, keys, KV_RANK), BF16),
                        pltpu.VMEM((rows, 1), F32), pltpu.VMEM((rows, 1), F32),
                        pltpu.VMEM((rows, KV_RANK), F32)],
    )
    return pl.pallas_call(
        _attn_sample_kernel,
        grid_spec=grid_spec,
        out_shape=jax.ShapeDtypeStruct((nseq, N_HEADS, DEC_SEQ, KV_RANK), F32),
        compiler_params=_params(2),
        name="attn_sample",
    )(page_table, q, q, knew, cache_ckv, cache_krope_t)


def _vup_sample_kernel(o_ref, wvb_ref, mla_ref):
    ns = PROJ_ROWS // DEC_SEQ
    for hh in range(N_HEADS):
        o_h = o_ref[:, hh, :, :].reshape(ns * DEC_SEQ, KV_RANK).astype(BF16)
        mla_ref[:, hh * V_HEAD_DIM:(hh + 1) * V_HEAD_DIM] = _dot(o_h, wvb_ref[hh]).astype(BF16)


def _vup_sample(o_lat, wvb):
    nseq = o_lat.shape[0]
    ns = PROJ_ROWS // DEC_SEQ
    return pl.pallas_call(
        _vup_sample_kernel,
        grid=(nseq // ns,),
        in_specs=[pl.BlockSpec((ns, N_HEADS, DEC_SEQ, KV_RANK), lambda i: (i, 0, 0, 0)),
                  _const_spec(wvb.shape)],
        out_specs=pl.BlockSpec((PROJ_ROWS, MLA_WIDTH), lambda i: (i, 0)),
        out_shape=jax.ShapeDtypeStruct((nseq * DEC_SEQ, MLA_WIDTH), BF16),
        compiler_params=_params(1),
        name="vup_sample",
    )(o_lat, wvb)


def _rope_tables(pos):
    inv = ROPE_THETA ** (-jnp.arange(0, QK_ROPE_DIM, 2, dtype=F32) / QK_ROPE_DIM)
    ang = pos.astype(F32)[:, None] * inv[None, :]
    cos, sin = jnp.cos(ang), jnp.sin(ang)
    reps = ROPE_PAD // QK_ROPE_DIM
    cs = jnp.tile(jnp.concatenate([cos, cos], axis=-1), (1, reps))
    sn = jnp.tile(jnp.concatenate([-sin, sin], axis=-1), (1, reps))
    return cs, sn


def _swap_halves(w):
    half = w.shape[-1] // 2
    return jnp.concatenate([w[..., half:], w[..., :half]], axis=-1)


def _pad_lanes(w, width):
    return jnp.pad(w, [(0, 0)] * (w.ndim - 1) + [(0, width - w.shape[-1])])


def _layout_weights(attn_norm, w_in, q_a_norm, w_q_b, kv_a_norm, w_k_b, w_pool, pool_scale):
    s1 = Q_RANK + KV_RANK
    s2 = s1 + QK_ROPE_DIM
    w_t = jnp.swapaxes(w_in, 0, 1)
    w_kr = w_t[s1:s2]
    half = QK_ROPE_DIM // 2
    zeros = jnp.zeros((ROPE_PAD - QK_ROPE_DIM, D_MODEL), w_in.dtype)
    w_rope = jnp.concatenate([w_kr, zeros, w_kr[half:], w_kr[:half], zeros], axis=0).astype(BF16)
    w_t = w_t.astype(BF16)
    wq_nope = w_q_b[:, :, :QK_NOPE_DIM].reshape(Q_RANK, Q_NOPE_W)
    wq_r = w_q_b[:, :, QK_NOPE_DIM:]
    wq_rope = _pad_lanes(wq_r, ROPE_PAD).reshape(Q_RANK, Q_ROPE_W)
    wq_swap = _pad_lanes(_swap_halves(wq_r), ROPE_PAD).reshape(Q_RANK, Q_ROPE_W)
    wq = jnp.concatenate([wq_nope, wq_rope, wq_swap], axis=1).astype(BF16)
    wkb = jnp.transpose(w_k_b, (1, 2, 0)).astype(BF16)
    zero = jnp.zeros_like(wkb[0::2])
    wkb = jnp.concatenate([jnp.concatenate([wkb[0::2], zero], axis=2),
                           jnp.concatenate([zero, wkb[1::2]], axis=2)], axis=1)
    return (attn_norm.reshape(1, D_MODEL), w_t, w_rope, q_a_norm.reshape(1, Q_RANK), wq,
            kv_a_norm.reshape(1, KV_RANK), wkb, w_pool.astype(BF16),
            pool_scale.reshape(1, POOL_WIDTH))


def kernel(x_prompt, x_sample, mem_prompt, cache_ckv, cache_krope, page_table, state_pool,
           cache_mem_k, cache_mem_v, attn_norm, mem_norm, w_in, q_a_norm, w_q_b, kv_a_norm,
           w_k_b, w_v_b, w_pool, pool_scale, w_mem_kv, w_out, final_norm):
    assert DEPTH == 1 and attn_norm.shape[0] == 1
    n_phys = cache_ckv.shape[1]
    wts = _layout_weights(attn_norm[0], w_in[0], q_a_norm[0], w_q_b[0], kv_a_norm[0], w_k_b[0],
                          w_pool[0], pool_scale[0])
    wvb = jnp.transpose(w_v_b[0], (1, 0, 2)).astype(BF16)
    wout = w_out[0].astype(BF16)
    fn = final_norm.reshape(1, D_MODEL)

    mk, mv, mkb, mvb = _mem_kv(mem_prompt, mem_norm[0].reshape(1, D_MODEL), w_mem_kv[0].astype(BF16))
    cs_p, sn_p = _rope_tables(jnp.arange(SEQ))
    q_p, ckv_p, kr_p, kc_p, pool_p, cross_p, sg_p, pst_p = _proj_prompt(
        x_prompt, cs_p, sn_p, wts, mkb, mvb)
    mla_p = _attn_prompt(q_p, kc_p, wvb)
    rows_p = BATCH * SEQ
    y_p = _combine(x_prompt.reshape(rows_p, D_MODEL), mla_p.reshape(rows_p, MLA_WIDTH),
                   pool_p.reshape(rows_p, POOL_WIDTH), cross_p.reshape(rows_p, MEM_WIDTH),
                   sg_p.reshape(rows_p, D_MIX), wout, fn)

    rows_s = DEC_BATCH * DEC_SEQ
    cs_s, sn_s = _rope_tables(PAST_LEN + jnp.arange(DEC_SEQ))
    xs = x_sample.reshape(rows_s, D_MODEL)
    q_s, ckv_s, kr_s, kc_s, pool_s, qm_s, sg_s, pst_s = _proj_sample(
        xs, cs_s, sn_s, wts, state_pool[0])
    cross_s = _cross_sample(
        qm_s,
        cache_mem_k.reshape(DEC_BATCH, MEM_TOKENS * MEM_HEADS, MEM_HEAD_DIM),
        cache_mem_v.reshape(DEC_BATCH, MEM_TOKENS * MEM_HEADS, MEM_HEAD_DIM))
    o_lat_s = _attn_sample(page_table, q_s, kc_s,
                           cache_ckv.reshape(n_phys, PAGE_SIZE, KV_RANK),
                           jnp.swapaxes(cache_krope.reshape(n_phys, PAGE_SIZE, QK_ROPE_DIM), 1, 2))
    mla_s = _vup_sample(o_lat_s, wvb)
    y_s = _combine(xs, mla_s, pool_s, cross_s, sg_s, wout, fn)

    return (y_p.reshape(BATCH, SEQ, D_MODEL),
            y_s.reshape(DEC_BATCH, DEC_SEQ, D_MODEL),
            ckv_p[None],
            kr_p[None],
            pst_p[None],
            mk.reshape(1, BATCH, MEM_TOKENS, MEM_HEADS, MEM_HEAD_DIM),
            mv.reshape(1, BATCH, MEM_TOKENS, MEM_HEADS, MEM_HEAD_DIM),
            ckv_s.reshape(1, DEC_BATCH, DEC_SEQ, KV_RANK),
            kr_s.reshape(1, DEC_BATCH, DEC_SEQ, QK_ROPE_DIM),
            pst_s[None])
```
